```python
import jax, jax.numpy as jnp
from jax import lax
import numpy as np

D_MODEL = 1024
BATCH = 16
SEQ = 2048
DEPTH = 4
DEC_BATCH = 16
DEC_SEQ = 16
PAST_LEN = 2048

CHUNK = 64
QBLOCK = 128
N_A = DEPTH // 2
N_B = DEPTH - N_A
H_A = 4
DK_A = D_MODEL // 2 // H_A
DV_A = D_MODEL // H_A
KD_A = H_A * DK_A
VD_A = H_A * DV_A
GATE_RANK = 16
GATE_TAU = 16.0
IN_A = 2 * KD_A + 2 * VD_A + GATE_RANK
H_B = 8
NOPE = 128
ROPE = 64
V_B = D_MODEL // H_B
VD_B = H_B * V_B
Q_LORA = 384
KV_LORA = 256
IN_B = Q_LORA + VD_B
ROPE_BASE = 10000.0
EPS = 1e-6

kernel_name = "yoco_gla_mla_streaming_step"


def rmsnorm(x, g):
    xf = x.astype(jnp.float32)
    y = xf * lax.rsqrt(jnp.mean(xf * xf, axis=-1, keepdims=True) + EPS)
    return (y * g.astype(jnp.float32)).astype(x.dtype)


def rope(x, pos):
    half = ROPE // 2
    inv = jnp.power(ROPE_BASE, -jnp.arange(half, dtype=jnp.float32) / half)
    ang = pos.astype(jnp.float32)[:, None] * inv[None, :]
    shp = (1, pos.shape[0]) + (1,) * (x.ndim - 3) + (half,)
    cos = jnp.cos(ang).reshape(shp)
    sin = jnp.sin(ang).reshape(shp)
    xf = x.astype(jnp.float32)
    x1, x2 = xf[..., :half], xf[..., half:]
    return jnp.concatenate([x1 * cos - x2 * sin, x2 * cos + x1 * sin], axis=-1).astype(x.dtype)


def gla_chunk_step(S, blk):
    q, k, v, g = blk
    C = q.shape[2]
    b = jnp.cumsum(g, axis=2)
    b_last = b[:, :, -1:, :]
    qe = q * jnp.exp(b)
    scores = jnp.einsum('bhtk,bhsk->bhts', qe, k * jnp.exp(-b))
    causal = jnp.tril(jnp.ones((C, C), dtype=bool))
    scores = jnp.where(causal, scores, 0.0)
    o = jnp.einsum('bhts,bhsv->bhtv', scores, v) + jnp.einsum('bhtk,bhkv->bhtv', qe, S)
    S_new = jnp.exp(b_last[:, :, 0, :])[..., None] * S + jnp.einsum(
        'bhsk,bhsv->bhkv', k * jnp.exp(b_last - b), v)
    return S_new, o


def gla_core(q, k, v, g, S0):
    B, H, T, _ = q.shape
    c = min(CHUNK, T)
    n = T // c

    def split(a):
        return a.reshape(B, H, n, c, a.shape[-1]).transpose(2, 0, 1, 3, 4)

    S, o = lax.scan(gla_chunk_step, S0, (split(q), split(k), split(v), split(g)))
    o = o.transpose(1, 2, 0, 3, 4).reshape(B, H, T, v.shape[-1])
    return o, S


def gla_layer(x, S0, norm_pre, w_in, w_gate2, b_gate, head_norm, w_out, norm_post):
    B, T, _ = x.shape
    h = rmsnorm(x, norm_pre)
    proj = h @ w_in
    q, k, v, gate, glr = jnp.split(
        proj, [KD_A, 2 * KD_A, 2 * KD_A + VD_A, 2 * KD_A + 2 * VD_A], axis=-1)
    z = (glr @ w_gate2 + b_gate).astype(jnp.float32)
    g = jax.nn.log_sigmoid(z) / GATE_TAU

    def heads(a, d):
        return a.astype(jnp.float32).reshape(B, T, H_A, d).transpose(0, 2, 1, 3)

    o, S = gla_core(heads(q, DK_A) * (DK_A ** -0.5), heads(k, DK_A), heads(v, DV_A),
                    heads(g, DK_A), S0.astype(jnp.float32))
    o = rmsnorm(o.transpose(0, 2, 1, 3), head_norm.reshape(H_A, DV_A)).reshape(B, T, VD_A)
    o = o.astype(x.dtype) * jax.nn.silu(gate)
    return x + rmsnorm(o @ w_out, norm_post), S


def shared_kv(x, pos, kv_norm, kv_w_down, kv_latent_norm):
    h = rmsnorm(x, kv_norm)
    lat = h @ kv_w_down
    ckv = rmsnorm(lat[..., :KV_LORA], kv_latent_norm)
    krope = rope(lat[..., KV_LORA:], pos)
    return ckv, krope


def expand_kv(ckv, krope, w_uk, w_uv):
    B, S, _ = ckv.shape
    k_nope = (ckv @ w_uk).reshape(B, S, H_B, NOPE)
    k = jnp.concatenate(
        [k_nope, jnp.broadcast_to(krope[:, :, None, :], (B, S, H_B, ROPE)).astype(k_nope.dtype)], axis=-1)
    v = (ckv @ w_uv).reshape(B, S, H_B, V_B)
    return k, v


def mla_attend(q, k, v, q_offset):
    B, T, H, D = q.shape
    S = k.shape[1]
    blk = min(QBLOCK, T)
    n = T // blk
    scale = D ** -0.5
    k_chunk = jnp.arange(S) // CHUNK
    qb = q.reshape(B, n, blk, H, D).transpose(1, 0, 2, 3, 4)

    def one(args):
        qi, i = args
        q_chunk = (q_offset + i * blk + jnp.arange(blk)) // CHUNK
        s = jnp.einsum('bqhd,bshd->bhqs', qi, k).astype(jnp.float32) * scale
        s = jnp.where(k_chunk[None, :] <= q_chunk[:, None], s, -jnp.inf)
        p = jax.nn.softmax(s, axis=-1).astype(v.dtype)
        return jnp.einsum('bhqs,bshd->bqhd', p, v)

    o = lax.map(one, (qb, jnp.arange(n)))
    return o.transpose(1, 0, 2, 3, 4).reshape(B, T, H, v.shape[-1])


def mla_layer(x, k, v, pos, q_offset, norm_pre, w_in, q_norm, w_uq, w_out, norm_post):
    B, T, _ = x.shape
    h = rmsnorm(x, norm_pre)
    proj = h @ w_in
    q_lat, gate = proj[..., :Q_LORA], proj[..., Q_LORA:]
    q = (rmsnorm(q_lat, q_norm) @ w_uq).reshape(B, T, H_B, NOPE + ROPE)
    q = jnp.concatenate([q[..., :NOPE], rope(q[..., NOPE:], pos)], axis=-1)
    o = mla_attend(q, k, v, q_offset).reshape(B, T, VD_B) * jax.nn.silu(gate)
    return x + rmsnorm(o @ w_out, norm_post)


def trunk(x, gla_state0, cache_ckv, cache_krope, pos_offset,
          a_norm_pre, a_w_in, a_w_gate2, a_b_gate, a_head_norm, a_w_out, a_norm_post,
          kv_norm, kv_w_down, kv_latent_norm, kv_w_uk, kv_w_uv,
          b_norm_pre, b_w_in, b_q_norm, b_w_uq, b_w_out, b_norm_post):
    T = x.shape[1]
    pos = pos_offset + jnp.arange(T)
    states = []
    ckv_new = krope_new = k_sh = v_sh = None
    for layer in range(DEPTH):
        if layer == N_A:
            ckv_new, krope_new = shared_kv(x, pos, kv_norm, kv_w_down, kv_latent_norm)
            if cache_ckv is None:
                ckv_all, krope_all = ckv_new, krope_new
            else:
                ckv_all = jnp.concatenate([cache_ckv.astype(ckv_new.dtype), ckv_new], axis=1)
                krope_all = jnp.concatenate([cache_krope.astype(krope_new.dtype), krope_new], axis=1)
            k_sh, v_sh = expand_kv(ckv_all, krope_all, kv_w_uk, kv_w_uv)
        if layer < N_A:
            x, S = gla_layer(x, gla_state0[layer], a_norm_pre[layer], a_w_in[layer], a_w_gate2[layer],
                             a_b_gate[layer], a_head_norm[layer], a_w_out[layer], a_norm_post[layer])
            states.append(S)
        else:
            j = layer - N_A
            x = mla_layer(x, k_sh, v_sh, pos, pos_offset, b_norm_pre[j], b_w_in[j], b_q_norm[j],
                          b_w_uq[j], b_w_out[j], b_norm_post[j])
    return x, jnp.stack(states), ckv_new, krope_new


def setup_inputs(seed: int = 0) -> dict:
    key = jax.random.key(seed)
    ks = jax.random.split(key, 24)
    f32 = jnp.float32

    def w(k, shape, fan_in):
        return jax.random.normal(k, shape, f32) * (fan_in ** -0.5)

    def g(k, shape):
        return 1.0 + 0.05 * jax.random.normal(k, shape, f32)

    return {
        "x_prompt": jax.random.normal(ks[0], (BATCH, SEQ, D_MODEL), f32),
        "x_sample": jax.random.normal(ks[1], (DEC_BATCH, DEC_SEQ, D_MODEL), f32),
        "state_gla": jax.random.normal(ks[2], (N_A, DEC_BATCH, H_A, DK_A, DV_A), f32),
        "cache_ckv": jax.random.normal(ks[3], (DEC_BATCH, PAST_LEN, KV_LORA), f32),
        "cache_krope": jax.random.normal(ks[4], (DEC_BATCH, PAST_LEN, ROPE), f32),
        "a_norm_pre": g(ks[5], (N_A, D_MODEL)),
        "a_w_in": w(ks[6], (N_A, D_MODEL, IN_A), D_MODEL),
        "a_w_gate2": w(ks[7], (N_A, GATE_RANK, KD_A), GATE_RANK),
        "a_b_gate": 0.1 * jax.random.normal(ks[8], (N_A, KD_A), f32),
        "a_head_norm": g(ks[9], (N_A, VD_A)),
        "a_w_out": w(ks[10], (N_A, VD_A, D_MODEL), VD_A),
        "a_norm_post": g(ks[11], (N_A, D_MODEL)),
        "kv_norm": g(ks[12], (D_MODEL,)),
        "kv_w_down": w(ks[13], (D_MODEL, KV_LORA + ROPE), D_MODEL),
        "kv_latent_norm": g(ks[14], (KV_LORA,)),
        "kv_w_uk": w(ks[15], (KV_LORA, H_B * NOPE), KV_LORA),
        "kv_w_uv": w(ks[16], (KV_LORA, VD_B), KV_LORA),
        "b_norm_pre": g(ks[17], (N_B, D_MODEL)),
        "b_w_in": w(ks[18], (N_B, D_MODEL, IN_B), D_MODEL),
        "b_q_norm": g(ks[19], (N_B, Q_LORA)),
        "b_w_uq": w(ks[20], (N_B, Q_LORA, H_B * (NOPE + ROPE)), Q_LORA),
        "b_w_out": w(ks[21], (N_B, VD_B, D_MODEL), VD_B),
        "b_norm_post": g(ks[22], (N_B, D_MODEL)),
    }


def reference(x_prompt, x_sample, state_gla, cache_ckv, cache_krope,
              a_norm_pre, a_w_in, a_w_gate2, a_b_gate, a_head_norm, a_w_out, a_norm_post,
              kv_norm, kv_w_down, kv_latent_norm, kv_w_uk, kv_w_uv,
              b_norm_pre, b_w_in, b_q_norm, b_w_uq, b_w_out, b_norm_post):
    weights = (a_norm_pre, a_w_in, a_w_gate2, a_b_gate, a_head_norm, a_w_out, a_norm_post,
               kv_norm, kv_w_down, kv_latent_norm, kv_w_uk, kv_w_uv,
               b_norm_pre, b_w_in, b_q_norm, b_w_uq, b_w_out, b_norm_post)
    state0 = jnp.zeros((N_A, x_prompt.shape[0], H_A, DK_A, DV_A), jnp.float32)
    y_prompt, st_p, ckv_prompt, krope_prompt = trunk(x_prompt, state0, None, None, 0, *weights)
    past = cache_ckv.shape[1]
    y_sample, st_s, ckv_sample, krope_sample = trunk(x_sample, state_gla, cache_ckv, cache_krope,
                                                     past, *weights)
    state_gla_prompt = st_p.astype(x_prompt.dtype)
    state_gla_sample = st_s.astype(state_gla.dtype)
    return (y_prompt, y_sample, state_gla_prompt, state_gla_sample,
            ckv_prompt, krope_prompt, ckv_sample, krope_sample)
```

```python
import functools

import jax
import jax.numpy as jnp
from jax import lax
from jax.experimental import pallas as pl
from jax.experimental.pallas import tpu as pltpu

F32 = jnp.float32
BF16 = jnp.bfloat16

D_MODEL = 1024
CHUNK = 64
N_A = 2
N_B = 2
H_A = 4
DK_A = 128
DV_A = 256
KD_A = H_A * DK_A
VD_A = H_A * DV_A
GATE_RANK = 16
GATE_TAU = 16.0
H_B = 8
NOPE = 128
ROPE = 64
HALF = ROPE // 2
V_B = 128
VD_B = H_B * V_B
Q_LORA = 384
KV_LORA = 256
ROPE_BASE = 10000.0
EPS = 1e-6

LANES = 128
VMEM_LIMIT = 56 * 1024 * 1024

GLA_ROWS = 256
KV_ROWS = 512
ATT_Q = 256
ATT_K = 256


def _dot(a, b):
    return jnp.dot(a, b, preferred_element_type=F32)


def _dot_nt(a, b):
    return lax.dot_general(a, b, (((1,), (1,)), ((), ())), preferred_element_type=F32)


def _dot_tn(a, b):
    return lax.dot_general(a, b, (((0,), (0,)), ((), ())), preferred_element_type=F32)


def _rms(xf, g):
    ms = jnp.mean(xf * xf, axis=-1, keepdims=True)
    return xf * lax.rsqrt(ms + EPS) * g


def _silu(g):
    return g * (1.0 / (1.0 + jnp.exp(-g)))


def _const_spec(shape):
    return pl.BlockSpec(shape, lambda *_: (0,) * len(shape))


def _gla_kernel(*refs, rows, chunk, has_state):
    if has_state:
        (x_ref, s0_ref, npre_ref, win_ref, wg2_ref, bg_ref, hn_ref, wout_ref, npost_ref,
         y_ref, sout_ref, st_sc, o_sc) = refs
    else:
        (x_ref, npre_ref, win_ref, wg2_ref, bg_ref, hn_ref, wout_ref, npost_ref,
         y_ref, sout_ref, st_sc, o_sc) = refs
    t = pl.program_id(1)

    @pl.when(t == 0)
    def _():
        for h in range(H_A):
            if has_state:
                st_sc[h] = s0_ref[0, h].T
            else:
                st_sc[h] = jnp.zeros((DV_A, DK_A), F32)

    x = x_ref[0]
    hb = _rms(x, npre_ref[...]).astype(BF16)
    proj = _dot(hb, win_ref[...])
    glr = proj[:, 2 * KD_A + 2 * VD_A:].astype(BF16)
    z = _dot(glr, wg2_ref[...]) + bg_ref[...]
    g = (jnp.minimum(z, 0.0) - jnp.log(1.0 + jnp.exp(-jnp.abs(z)))) * (1.0 / GATE_TAU)

    shift = chunk.bit_length() - 1
    ri = lax.broadcasted_iota(jnp.int32, (rows, rows), 0)
    ci = lax.broadcasted_iota(jnp.int32, (rows, rows), 1)
    same = jnp.right_shift(ri, shift) == jnp.right_shift(ci, shift)
    tri = jnp.where(same, jnp.where(ci <= ri, 1.0, 0.0), 0.0).astype(BF16)
    g_hi = g.astype(BF16)
    g_lo = (g - g_hi.astype(F32)).astype(BF16)
    b = _dot(tri, g_hi) + _dot(tri, g_lo)

    rc = lax.broadcasted_iota(jnp.int32, (chunk, chunk), 0)
    cc = lax.broadcasted_iota(jnp.int32, (chunk, chunk), 1)
    causal = cc <= rc
    scale = DK_A ** -0.5

    for c in range(rows // chunk):
        r0, r1 = c * chunk, (c + 1) * chunk
        for h in range(H_A):
            bc = b[r0:r1, h * DK_A:(h + 1) * DK_A]
            bl = bc[chunk - 1:chunk, :]
            qh = proj[r0:r1, h * DK_A:(h + 1) * DK_A]
            kh = proj[r0:r1, KD_A + h * DK_A:KD_A + (h + 1) * DK_A]
            vh = proj[r0:r1, 2 * KD_A + h * DV_A:2 * KD_A + (h + 1) * DV_A].astype(BF16)
            qe = (qh * scale * jnp.exp(bc)).astype(BF16)
            ke = (kh * jnp.exp(-bc)).astype(BF16)
            kd = (kh * jnp.exp(bl - bc)).astype(BF16)
            s = jnp.where(causal, _dot_nt(qe, ke), 0.0).astype(BF16)
            st = st_sc[h]
            o_sc[r0:r1, h * DV_A:(h + 1) * DV_A] = _dot(s, vh) + _dot_nt(qe, st.astype(BF16))
            st_sc[h] = st * jnp.exp(bl) + _dot_tn(vh, kd)

    gate = proj[:, 2 * KD_A + VD_A:2 * KD_A + 2 * VD_A]
    heads = []
    for h in range(H_A):
        heads.append(_rms(o_sc[:, h * DV_A:(h + 1) * DV_A], hn_ref[:, h * DV_A:(h + 1) * DV_A]))
    og = (jnp.concatenate(heads, axis=1) * _silu(gate)).astype(BF16)
    y = _dot(og, wout_ref[...])
    y_ref[0] = x + _rms(y, npost_ref[...])

    @pl.when(t == pl.num_programs(1) - 1)
    def _():
        for h in range(H_A):
            sout_ref[0, h] = st_sc[h].T


def _gla_layer(x, s0, npre, win, wg2, bg, hn, wout, npost, *, rows, chunk):
    bsz, tlen, d = x.shape
    has_state = s0 is not None
    tok = pl.BlockSpec((1, rows, d), lambda b, t: (b, t, 0))
    st_spec = pl.BlockSpec((1, H_A, DK_A, DV_A), lambda b, t: (b, 0, 0, 0))
    in_specs = [tok] + ([st_spec] if has_state else []) + [
        _const_spec(npre.shape), _const_spec(win.shape), _const_spec(wg2.shape), _const_spec(bg.shape),
        _const_spec(hn.shape), _const_spec(wout.shape), _const_spec(npost.shape)]
    args = [x] + ([s0] if has_state else []) + [npre, win, wg2, bg, hn, wout, npost]
    return pl.pallas_call(
        functools.partial(_gla_kernel, rows=rows, chunk=chunk, has_state=has_state),
        grid=(bsz, tlen // rows),
        in_specs=in_specs,
        out_specs=(tok, st_spec),
        out_shape=(jax.ShapeDtypeStruct((bsz, tlen, d), F32),
                   jax.ShapeDtypeStruct((bsz, H_A, DK_A, DV_A), F32)),
        scratch_shapes=[pltpu.VMEM((H_A, DV_A, DK_A), F32), pltpu.VMEM((rows, VD_A), F32)],
        compiler_params=pltpu.CompilerParams(
            dimension_semantics=("parallel", "arbitrary"), vmem_limit_bytes=VMEM_LIMIT),
        name="gla_layer",
    )(*args)


def _kv_kernel(x_ref, cos_ref, sin_ref, nkv_ref, wdc_ref, wdr_ref, nlat_ref, *rest, expand):
    if expand:
        wuk_ref, wuv_ref, ckv_ref, kr_ref, kn_ref, krab_ref, v_ref = rest
    else:
        ckv_ref, kr_ref = rest
    hb = _rms(x_ref[0], nkv_ref[...]).astype(BF16)
    ckv = _rms(_dot(hb, wdc_ref[...]), nlat_ref[...])
    ckv_ref[0] = ckv
    rr = _dot(hb, wdr_ref[...])
    kr2 = rr[:, :LANES] * cos_ref[...] + rr[:, LANES:] * sin_ref[...]
    kr_ref[0] = kr2[:, :ROPE]
    if expand:
        cb = ckv.astype(BF16)
        kn_ref[0] = _dot(cb, wuk_ref[...]).astype(BF16)
        v_ref[0] = _dot(cb, wuv_ref[...]).astype(BF16)
        lane = lax.broadcasted_iota(jnp.int32, kr2.shape, 1)
        krab_ref[0] = jnp.concatenate(
            [jnp.where(lane < ROPE, kr2, 0.0), jnp.where(lane >= ROPE, kr2, 0.0)], axis=1).astype(BF16)


def _shared_kv(x, cos2, sin2, nkv, wdc, wdr, nlat, wuk=None, wuv=None, *, rows):
    bsz, tlen, d = x.shape
    expand = wuk is not None
    tok = lambda w: pl.BlockSpec((1, rows, w), lambda b, t: (b, t, 0))
    tab = pl.BlockSpec((rows, LANES), lambda b, t: (t, 0))
    in_specs = [tok(d), tab, tab, _const_spec(nkv.shape), _const_spec(wdc.shape),
                _const_spec(wdr.shape), _const_spec(nlat.shape)]
    args = [x, cos2, sin2, nkv, wdc, wdr, nlat]
    out_specs = [tok(KV_LORA), tok(ROPE)]
    out_shape = [jax.ShapeDtypeStruct((bsz, tlen, KV_LORA), F32),
                 jax.ShapeDtypeStruct((bsz, tlen, ROPE), F32)]
    if expand:
        in_specs += [_const_spec(wuk.shape), _const_spec(wuv.shape)]
        args += [wuk, wuv]
        out_specs += [tok(H_B * NOPE), tok(2 * LANES), tok(VD_B)]
        out_shape += [jax.ShapeDtypeStruct((bsz, tlen, H_B * NOPE), BF16),
                      jax.ShapeDtypeStruct((bsz, tlen, 2 * LANES), BF16),
                      jax.ShapeDtypeStruct((bsz, tlen, VD_B), BF16)]
    return pl.pallas_call(
        functools.partial(_kv_kernel, expand=expand),
        grid=(bsz, tlen // rows),
        in_specs=in_specs,
        out_specs=tuple(out_specs),
        out_shape=tuple(out_shape),
        compiler_params=pltpu.CompilerParams(
            dimension_semantics=("parallel", "parallel"), vmem_limit_bytes=VMEM_LIMIT),
        name="shared_kv",
    )(*args)


def _mla_q(x, cos, sin, npre_ref, win_ref, qn_ref, wq_ref, rope_w):
    hb = _rms(x, npre_ref[...]).astype(BF16)
    proj = _dot(hb, win_ref[...])
    qn = _rms(proj[:, :Q_LORA], qn_ref[...]).astype(BF16)
    qq = _dot(qn, wq_ref[...])
    scale = (NOPE + ROPE) ** -0.5
    n0, n1 = H_B * NOPE, H_B * NOPE + H_B * rope_w
    q_nope = (qq[:, :n0] * scale).astype(BF16)
    q_rope = ((qq[:, n0:n1] * cos + qq[:, n1:] * sin) * scale).astype(BF16)
    return q_nope, q_rope, proj[:, Q_LORA:]


def _mla_kernel(x_ref, kn_ref, kr_ref, v_ref, cos_ref, sin_ref, npre_ref, win_ref, qn_ref, wq_ref,
                wout_ref, npost_ref, y_ref, q_sc, m_sc, acc_sc, *, tq, tk):
    i = pl.program_id(1)
    x = x_ref[0]
    q_nope, q_rope, gate = _mla_q(x, cos_ref[...], sin_ref[...], npre_ref, win_ref, qn_ref, wq_ref, ROPE)
    q_sc[:, :H_B * NOPE] = q_nope
    q_sc[:, H_B * NOPE:] = q_rope
    m_sc[...] = jnp.full(m_sc.shape, -jnp.inf, F32)
    acc_sc[...] = jnp.zeros(acc_sc.shape, F32)

    ones = jnp.ones((tk, LANES), BF16)
    rq = lax.broadcasted_iota(jnp.int32, (tq, tk), 0)
    ck = lax.broadcasted_iota(jnp.int32, (tq, tk), 1)
    cshift = CHUNK.bit_length() - 1

    def step(j, diagonal):
        k0 = pl.multiple_of(j * tk, tk)
        if diagonal:
            visible = jnp.right_shift(ck, cshift) <= jnp.right_shift(rq, cshift)
        for h in range(H_B):
            qh = jnp.concatenate(
                [q_sc[:, h * NOPE:(h + 1) * NOPE],
                 q_sc[:, H_B * NOPE + (h // 2) * LANES:H_B * NOPE + (h // 2 + 1) * LANES]], axis=1)
            kb = jnp.concatenate(
                [kn_ref[0, pl.ds(k0, tk), h * NOPE:(h + 1) * NOPE],
                 kr_ref[0, pl.ds(k0, tk), (h % 2) * LANES:(h % 2 + 1) * LANES]], axis=1)
            s = _dot_nt(qh, kb)
            if diagonal:
                s = jnp.where(visible, s, -jnp.inf)
            m_prev = m_sc[h]
            m_new = jnp.maximum(m_prev, jnp.max(s, axis=-1, keepdims=True))
            alpha = jnp.exp(m_prev - m_new)
            p = jnp.exp(s - jnp.concatenate([m_new] * (tk // LANES), axis=1)).astype(BF16)
            vb = jnp.concatenate([v_ref[0, pl.ds(k0, tk), h * V_B:(h + 1) * V_B], ones], axis=1)
            acc_sc[h] = jnp.concatenate([alpha, alpha], axis=1) * acc_sc[h] + _dot(p, vb)
            m_sc[h] = m_new

    def body(j, carry):
        step(j, False)
        return carry

    lax.fori_loop(0, i, body, 0)
    step(i, True)

    outs = []
    for h in range(H_B):
        a = acc_sc[h]
        outs.append(a[:, :V_B] * (1.0 / a[:, V_B:]))
    og = (jnp.concatenate(outs, axis=1) * _silu(gate)).astype(BF16)
    y = _dot(og, wout_ref[...])
    y_ref[0] = x + _rms(y, npost_ref[...])


def _mla_prompt_layer(x, kn, krab, v, cos8, sin8, npre, win, qn, wq, wout, npost, *, tq, tk):
    bsz, tlen, d = x.shape
    assert tq == tk and tq % CHUNK == 0
    tok = pl.BlockSpec((1, tq, d), lambda b, t: (b, t, 0))
    full = lambda w: pl.BlockSpec((1, tlen, w), lambda b, t: (b, 0, 0))
    tab = pl.BlockSpec((tq, cos8.shape[1]), lambda b, t: (t, 0))
    return pl.pallas_call(
        functools.partial(_mla_kernel, tq=tq, tk=tk),
        grid=(bsz, tlen // tq),
        in_specs=[tok, full(kn.shape[2]), full(krab.shape[2]), full(v.shape[2]), tab, tab,
                  _const_spec(npre.shape), _const_spec(win.shape), _const_spec(qn.shape),
                  _const_spec(wq.shape), _const_spec(wout.shape), _const_spec(npost.shape)],
        out_specs=tok,
        out_shape=jax.ShapeDtypeStruct((bsz, tlen, d), F32),
        scratch_shapes=[pltpu.VMEM((tq, H_B * (NOPE + ROPE)), BF16),
                        pltpu.VMEM((H_B, tq, LANES), F32),
                        pltpu.VMEM((H_B, tq, 2 * LANES), F32)],
        compiler_params=pltpu.CompilerParams(
            dimension_semantics=("parallel", "arbitrary"), vmem_limit_bytes=VMEM_LIMIT),
        name="mla_prompt_layer",
    )(x, kn, krab, v, cos8, sin8, npre, win, qn, wq, wout, npost)


def _mla_sample_kernel(x_ref, cckv_ref, ckr_ref, nckv_ref, nkr_ref, cos_ref, sin_ref, npre_ref, win_ref,
                       qn_ref, wq_ref, wuk_ref, wuv_ref, wout_ref, npost_ref, y_ref, *, past):
    x = x_ref[0]
    t = x.shape[0]
    q_nope, q_rope, gate = _mla_q(x, cos_ref[...], sin_ref[...], npre_ref, win_ref, qn_ref, wq_ref, LANES)

    qlat = jnp.concatenate(
        [_dot_nt(q_nope[:, h * NOPE:(h + 1) * NOPE], wuk_ref[:, h * NOPE:(h + 1) * NOPE]) for h in range(H_B)],
        axis=0).astype(BF16)
    qr = jnp.concatenate([q_rope[:, h * LANES:h * LANES + ROPE] for h in range(H_B)], axis=0)

    c_ckv = cckv_ref[0].astype(BF16)
    n_ckv = nckv_ref[0].astype(BF16)
    s_c = _dot_nt(qlat, c_ckv) + _dot_nt(qr, ckr_ref[0].astype(BF16))
    s_n = _dot_nt(qlat, n_ckv) + _dot_nt(qr, nkr_ref[0].astype(BF16))

    cshift = CHUNK.bit_length() - 1

    def chunk_mask(s, k_start):
        rows = lax.broadcasted_iota(jnp.int32, s.shape, 0)
        cols = lax.broadcasted_iota(jnp.int32, s.shape, 1)
        q_chunk = jnp.right_shift(past + jnp.bitwise_and(rows, t - 1), cshift)
        k_chunk = jnp.right_shift(k_start + cols, cshift)
        return jnp.where(k_chunk <= q_chunk, s, -jnp.inf)

    s_c = chunk_mask(s_c, 0)
    s_n = chunk_mask(s_n, past)
    m = jnp.maximum(jnp.max(s_c, axis=-1, keepdims=True), jnp.max(s_n, axis=-1, keepdims=True))
    p_c = jnp.exp(s_c - m)
    p_n = jnp.exp(s_n - m)
    l = jnp.sum(p_c, axis=-1, keepdims=True) + jnp.sum(p_n, axis=-1, keepdims=True)
    olat = (_dot(p_c.astype(BF16), c_ckv) + _dot(p_n.astype(BF16), n_ckv)) * (1.0 / l)
    olat = olat.astype(BF16)
    o = jnp.concatenate(
        [_dot(olat[h * t:(h + 1) * t], wuv_ref[:, h * V_B:(h + 1) * V_B]) for h in range(H_B)], axis=1)
    og = (o * _silu(gate)).astype(BF16)
    y = _dot(og, wout_ref[...])
    y_ref[0] = x + _rms(y, npost_ref[...])


def _mla_sample_layer(x, cache_ckv, cache_kr, new_ckv, new_kr, cos, sin, npre, win, qn, wq, wuk, wuv,
                      wout, npost):
    bsz, t, d = x.shape
    past = cache_ckv.shape[1]
    assert t & (t - 1) == 0
    per_b = lambda a: pl.BlockSpec((1,) + a.shape[1:], lambda b: (b, 0, 0))
    consts = [cos, sin, npre, win, qn, wq, wuk, wuv, wout, npost]
    return pl.pallas_call(
        functools.partial(_mla_sample_kernel, past=past),
        grid=(bsz,),
        in_specs=[per_b(x), per_b(cache_ckv), per_b(cache_kr), per_b(new_ckv), per_b(new_kr)]
                 + [_const_spec(a.shape) for a in consts],
        out_specs=per_b(x),
        out_shape=jax.ShapeDtypeStruct((bsz, t, d), F32),
        compiler_params=pltpu.CompilerParams(
            dimension_semantics=("parallel",), vmem_limit_bytes=VMEM_LIMIT),
        name="mla_sample_layer",
    )(x, cache_ckv, cache_kr, new_ckv, new_kr, *consts)


def _rope_tables(pos_offset, tlen):
    inv = jnp.power(ROPE_BASE, -jnp.arange(HALF, dtype=F32) / HALF)
    ang = (pos_offset + jnp.arange(tlen)).astype(F32)[:, None] * inv[None, :]
    cos = jnp.cos(ang)
    sin = jnp.sin(ang)
    return jnp.concatenate([cos, cos], axis=1), jnp.concatenate([sin, sin], axis=1)


def _rot_cols(w):
    return jnp.concatenate([-w[..., HALF:], w[..., :HALF]], axis=-1)


def _row(a):
    return a.reshape(1, -1)


def kernel(x_prompt, x_sample, state_gla, cache_ckv, cache_krope, a_norm_pre, a_w_in, a_w_gate2, a_b_gate,
           a_head_norm, a_w_out, a_norm_post, kv_norm, kv_w_down, kv_latent_norm, kv_w_uk, kv_w_uv,
           b_norm_pre, b_w_in, b_q_norm, b_w_uq, b_w_out, b_norm_post):
    past = cache_ckv.shape[1]
    t_p, t_s = x_prompt.shape[1], x_sample.shape[1]

    a_win = a_w_in.astype(BF16)
    a_wg2 = a_w_gate2.astype(BF16)
    a_wout = a_w_out.astype(BF16)
    wdc = kv_w_down[:, :KV_LORA].astype(BF16)
    wdr_cols = kv_w_down[:, KV_LORA:]
    wdr = jnp.concatenate([wdr_cols, wdr_cols, _rot_cols(wdr_cols), _rot_cols(wdr_cols)], axis=1).astype(BF16)
    wuk = kv_w_uk.astype(BF16)
    wuv = kv_w_uv.astype(BF16)
    b_win = b_w_in.astype(BF16)
    b_wout = b_w_out.astype(BF16)
    uq = b_w_uq.reshape(N_B, Q_LORA, H_B, NOPE + ROPE)
    uq_nope = uq[..., :NOPE].reshape(N_B, Q_LORA, H_B * NOPE)
    uq_rope = uq[..., NOPE:]
    uq_rot = _rot_cols(uq_rope)
    wq_p = jnp.concatenate([uq_nope, uq_rope.reshape(N_B, Q_LORA, H_B * ROPE),
                            uq_rot.reshape(N_B, Q_LORA, H_B * ROPE)], axis=-1).astype(BF16)
    pad = lambda w: jnp.pad(w, ((0, 0), (0, 0), (0, 0), (0, LANES - ROPE))).reshape(N_B, Q_LORA, H_B * LANES)
    wq_s = jnp.concatenate([uq_nope, pad(uq_rope), pad(uq_rot)], axis=-1).astype(BF16)

    cos_p, sin_p = _rope_tables(0, t_p)
    cos_s, sin_s = _rope_tables(past, t_s)
    tile2 = lambda a: jnp.concatenate([a, a], axis=1)
    cos8_p, sin8_p = jnp.tile(cos_p, (1, H_B)), jnp.tile(sin_p, (1, H_B))
    padl = lambda a: jnp.tile(jnp.pad(a, ((0, 0), (0, LANES - ROPE))), (1, H_B))
    cos8_s, sin8_s = padl(cos_s), padl(sin_s)

    def gla_stack(x, states, rows, chunk):
        outs = []
        for l in range(N_A):
            s0 = None if states is None else states[l]
            x, s_new = _gla_layer(x, s0, _row(a_norm_pre[l]), a_win[l], a_wg2[l], _row(a_b_gate[l]),
                                  _row(a_head_norm[l]), a_wout[l], _row(a_norm_post[l]),
                                  rows=rows, chunk=chunk)
            outs.append(s_new)
        return x, jnp.stack(outs)

    x, st_p = gla_stack(x_prompt, None, GLA_ROWS, CHUNK)
    ckv_p, kr_p, kn, krab, v = _shared_kv(x, tile2(cos_p), tile2(sin_p), _row(kv_norm), wdc, wdr,
                                          _row(kv_latent_norm), wuk, wuv, rows=KV_ROWS)
    for j in range(N_B):
        x = _mla_prompt_layer(x, kn, krab, v, cos8_p, sin8_p, _row(b_norm_pre[j]), b_win[j],
                              _row(b_q_norm[j]), wq_p[j], b_wout[j], _row(b_norm_post[j]),
                              tq=ATT_Q, tk=ATT_K)
    y_prompt = x

    x, st_s = gla_stack(x_sample, state_gla, t_s, min(CHUNK, t_s))
    ckv_s, kr_s = _shared_kv(x, tile2(cos_s), tile2(sin_s), _row(kv_norm), wdc, wdr,
                             _row(kv_latent_norm), rows=t_s)
    for j in range(N_B):
        x = _mla_sample_layer(x, cache_ckv, cache_krope, ckv_s, kr_s, cos8_s, sin8_s,
                              _row(b_norm_pre[j]), b_win[j], _row(b_q_norm[j]), wq_s[j], wuk, wuv,
                              b_wout[j], _row(b_norm_post[j]))
    y_sample = x

    return (y_prompt, y_sample, st_p, st_s, ckv_p, kr_p, ckv_s, kr_s)
```

```python
import functools

import jax
import jax.numpy as jnp
from jax import lax
from jax.experimental import pallas as pl
from jax.experimental.pallas import tpu as pltpu

F32 = jnp.float32
BF16 = jnp.bfloat16

D_MODEL = 1024
CHUNK = 64
N_A = 2
N_B = 2
H_A = 4
DK_A = 128
DV_A = 256
KD_A = H_A * DK_A
VD_A = H_A * DV_A
GATE_RANK = 16
GATE_TAU = 16.0
H_B = 8
NOPE = 128
ROPE = 64
HALF = ROPE // 2
V_B = 128
VD_B = H_B * V_B
Q_LORA = 384
KV_LORA = 256
ROPE_BASE = 10000.0
EPS = 1e-6

LANES = 128
VMEM_LIMIT = 56 * 1024 * 1024

GLA_ROWS = 256
KV_ROWS = 512
ATT_Q = 256
ATT_K = 256


def _dot(a, b):
    return jnp.dot(a, b, preferred_element_type=F32)


def _dot_nt(a, b):
    return lax.dot_general(a, b, (((1,), (1,)), ((), ())), preferred_element_type=F32)


def _dot_tn(a, b):
    return lax.dot_general(a, b, (((0,), (0,)), ((), ())), preferred_element_type=F32)


def _rms(xf, g):
    ms = jnp.mean(xf * xf, axis=-1, keepdims=True)
    return xf * lax.rsqrt(ms + EPS) * g


def _silu(g):
    return g * (1.0 / (1.0 + jnp.exp(-g)))


def _const_spec(shape):
    return pl.BlockSpec(shape, lambda *_: (0,) * len(shape))


def _gla_kernel(*refs, rows, chunk, has_state):
    if has_state:
        (x_ref, s0_ref, npre_ref, win_ref, wg2_ref, bg_ref, hn_ref, wout_ref, npost_ref,
         y_ref, sout_ref, st_sc, o_sc, qe_sc, u_sc, stb_sc) = refs
    else:
        (x_ref, npre_ref, win_ref, wg2_ref, bg_ref, hn_ref, wout_ref, npost_ref,
         y_ref, sout_ref, st_sc, o_sc, qe_sc, u_sc, stb_sc) = refs
    t = pl.program_id(1)

    @pl.when(t == 0)
    def _():
        for h in range(H_A):
            if has_state:
                st_sc[h] = s0_ref[0, h].T
            else:
                st_sc[h] = jnp.zeros((DV_A, DK_A), F32)

    x = x_ref[0]
    hb = _rms(x, npre_ref[...]).astype(BF16)
    n_main = 2 * KD_A + 2 * VD_A
    glr = _dot(hb, win_ref[:, n_main:]).astype(BF16)
    z = _dot(glr, wg2_ref[...]) + bg_ref[...]
    g = (jnp.minimum(z, 0.0) - jnp.log(1.0 + jnp.exp(-jnp.abs(z)))) * (1.0 / GATE_TAU)

    shift = chunk.bit_length() - 1
    ri = lax.broadcasted_iota(jnp.int32, (rows, rows), 0)
    ci = lax.broadcasted_iota(jnp.int32, (rows, rows), 1)
    inchunk = jnp.where(jnp.right_shift(ri, shift) == jnp.right_shift(ci, shift), ci, rows) <= ri
    tri = jnp.where(inchunk, 1.0, 0.0).astype(BF16)
    g_hi = g.astype(BF16)
    g_lo = (g - g_hi.astype(F32)).astype(BF16)
    b = _dot(tri, g_hi) + _dot(tri, g_lo)

    proj = _dot(hb, win_ref[:, :n_main])
    scale = DK_A ** -0.5
    nc = rows // chunk
    decay = [[None] * nc for _ in range(H_A)]

    for h in range(H_A):
        bh = b[:, h * DK_A:(h + 1) * DK_A]
        ends = [bh[(c + 1) * chunk - 1:(c + 1) * chunk, :] for c in range(nc)]
        bl = jnp.concatenate([jnp.broadcast_to(e, (chunk, DK_A)) for e in ends], axis=0)
        qh = proj[:, h * DK_A:(h + 1) * DK_A]
        kh = proj[:, KD_A + h * DK_A:KD_A + (h + 1) * DK_A]
        vh = proj[:, 2 * KD_A + h * DV_A:2 * KD_A + (h + 1) * DV_A].astype(BF16)
        qe = (qh * scale * jnp.exp(bh)).astype(BF16)
        ke = (kh * jnp.exp(-bh)).astype(BF16)
        kd = (kh * jnp.exp(bl - bh)).astype(BF16)
        s = jnp.where(inchunk, _dot_nt(qe, ke), 0.0).astype(BF16)
        o_sc[:, h * DV_A:(h + 1) * DV_A] = _dot(s, vh)
        qe_sc[:, h * DK_A:(h + 1) * DK_A] = qe
        for c in range(nc):
            r0, r1 = c * chunk, (c + 1) * chunk
            u_sc[h, c] = _dot_tn(vh[r0:r1], kd[r0:r1])
            decay[h][c] = jnp.exp(ends[c])

    for h in range(H_A):
        st = st_sc[h]
        for c in range(nc):
            stb_sc[h, c] = st.astype(BF16)
            st = st * decay[h][c] + u_sc[h, c]
        st_sc[h] = st

    for c in range(nc):
        r0, r1 = c * chunk, (c + 1) * chunk
        for h in range(H_A):
            o_sc[r0:r1, h * DV_A:(h + 1) * DV_A] += _dot_nt(qe_sc[r0:r1, h * DK_A:(h + 1) * DK_A], stb_sc[h, c])

    gate = proj[:, 2 * KD_A + VD_A:2 * KD_A + 2 * VD_A]
    heads = []
    for h in range(H_A):
        heads.append(_rms(o_sc[:, h * DV_A:(h + 1) * DV_A], hn_ref[:, h * DV_A:(h + 1) * DV_A]))
    og = (jnp.concatenate(heads, axis=1) * _silu(gate)).astype(BF16)
    y = _dot(og, wout_ref[...])
    y_ref[0] = x + _rms(y, npost_ref[...])

    @pl.when(t == pl.num_programs(1) - 1)
    def _():
        for h in range(H_A):
            sout_ref[0, h] = st_sc[h].T


def _gla_layer(x, s0, npre, win, wg2, bg, hn, wout, npost, *, rows, chunk):
    bsz, tlen, d = x.shape
    has_state = s0 is not None
    tok = pl.BlockSpec((1, rows, d), lambda b, t: (b, t, 0))
    st_spec = pl.BlockSpec((1, H_A, DK_A, DV_A), lambda b, t: (b, 0, 0, 0))
    in_specs = [tok] + ([st_spec] if has_state else []) + [
        _const_spec(npre.shape), _const_spec(win.shape), _const_spec(wg2.shape), _const_spec(bg.shape),
        _const_spec(hn.shape), _const_spec(wout.shape), _const_spec(npost.shape)]
    args = [x] + ([s0] if has_state else []) + [npre, win, wg2, bg, hn, wout, npost]
    return pl.pallas_call(
        functools.partial(_gla_kernel, rows=rows, chunk=chunk, has_state=has_state),
        grid=(bsz, tlen // rows),
        in_specs=in_specs,
        out_specs=(tok, st_spec),
        out_shape=(jax.ShapeDtypeStruct((bsz, tlen, d), F32),
                   jax.ShapeDtypeStruct((bsz, H_A, DK_A, DV_A), F32)),
        scratch_shapes=[pltpu.VMEM((H_A, DV_A, DK_A), F32), pltpu.VMEM((rows, VD_A), F32),
                        pltpu.VMEM((rows, KD_A), BF16),
                        pltpu.VMEM((H_A, rows // chunk, DV_A, DK_A), F32),
                        pltpu.VMEM((H_A, rows // chunk, DV_A, DK_A), BF16)],
        compiler_params=pltpu.CompilerParams(
            dimension_semantics=("parallel", "arbitrary"), vmem_limit_bytes=VMEM_LIMIT),
        name="gla_layer",
    )(*args)


def _kv_kernel(x_ref, cos_ref, sin_ref, nkv_ref, wdc_ref, wdr_ref, nlat_ref, *rest, expand, kblk):
    if expand:
        wuk_ref, wuvt_ref, ckv_ref, kr_ref, kn_ref, kr0_ref, vt_ref = rest
    else:
        ckv_ref, kr_ref = rest
    hb = _rms(x_ref[0], nkv_ref[...]).astype(BF16)
    ckv = _rms(_dot(hb, wdc_ref[...]), nlat_ref[...])
    ckv_ref[0] = ckv
    rr = _dot(hb, wdr_ref[...])
    kr2 = rr[:, :LANES] * cos_ref[...] + rr[:, LANES:] * sin_ref[...]
    kr_ref[0] = kr2[:, :ROPE]
    if expand:
        cb = ckv.astype(BF16)
        kn_ref[0] = _dot(cb, wuk_ref[...]).astype(BF16)
        lane = lax.broadcasted_iota(jnp.int32, kr2.shape, 1)
        kr0_ref[0] = jnp.where(lane < ROPE, kr2, 0.0).astype(BF16)
        vt = _dot_nt(wuvt_ref[...], cb).astype(BF16)
        for i in range(vt.shape[1] // kblk):
            vt_ref[0, i] = vt[:, i * kblk:(i + 1) * kblk]


def _shared_kv(x, cos2, sin2, nkv, wdc, wdr, nlat, wuk=None, wuvt=None, *, rows, kblk=None):
    bsz, tlen, d = x.shape
    expand = wuk is not None
    tok = lambda w: pl.BlockSpec((1, rows, w), lambda b, t: (b, t, 0))
    tab = pl.BlockSpec((rows, LANES), lambda b, t: (t, 0))
    in_specs = [tok(d), tab, tab, _const_spec(nkv.shape), _const_spec(wdc.shape),
                _const_spec(wdr.shape), _const_spec(nlat.shape)]
    args = [x, cos2, sin2, nkv, wdc, wdr, nlat]
    out_specs = [tok(KV_LORA), tok(ROPE)]
    out_shape = [jax.ShapeDtypeStruct((bsz, tlen, KV_LORA), F32),
                 jax.ShapeDtypeStruct((bsz, tlen, ROPE), F32)]
    if expand:
        in_specs += [_const_spec(wuk.shape), _const_spec(wuvt.shape)]
        args += [wuk, wuvt]
        out_specs += [tok(H_B * NOPE), tok(LANES),
                      pl.BlockSpec((1, rows // kblk, VD_B, kblk), lambda b, t: (b, t, 0, 0))]
        out_shape += [jax.ShapeDtypeStruct((bsz, tlen, H_B * NOPE), BF16),
                      jax.ShapeDtypeStruct((bsz, tlen, LANES), BF16),
                      jax.ShapeDtypeStruct((bsz, tlen // kblk, VD_B, kblk), BF16)]
    return pl.pallas_call(
        functools.partial(_kv_kernel, expand=expand, kblk=kblk),
        grid=(bsz, tlen // rows),
        in_specs=in_specs,
        out_specs=tuple(out_specs),
        out_shape=tuple(out_shape),
        compiler_params=pltpu.CompilerParams(
            dimension_semantics=("parallel", "parallel"), vmem_limit_bytes=VMEM_LIMIT),
        name="shared_kv",
    )(*args)


def _mla_q(x, cos, sin, npre_ref, win_ref, qn_ref, wq_ref, rope_w):
    hb = _rms(x, npre_ref[...]).astype(BF16)
    proj = _dot(hb, win_ref[...])
    qn = _rms(proj[:, :Q_LORA], qn_ref[...]).astype(BF16)
    qq = _dot(qn, wq_ref[...])
    scale = (NOPE + ROPE) ** -0.5
    n0, n1 = H_B * NOPE, H_B * NOPE + H_B * rope_w
    q_nope = (qq[:, :n0] * scale).astype(BF16)
    q_rope = ((qq[:, n0:n1] * cos + qq[:, n1:] * sin) * scale).astype(BF16)
    return q_nope, q_rope, proj[:, Q_LORA:]


ACC_ROWS = V_B + 16


def _mla_kernel(x_ref, kn_ref, kr_ref, vt_ref, cos_ref, sin_ref, npre_ref, win_ref, qn_ref, wqt_ref,
                wout_ref, npost_ref, y_ref, q_sc, m_sc, acc_sc, s_sc, p_sc, *, tq, tk):
    i = pl.program_id(1)
    x = x_ref[0]
    hb = _rms(x, npre_ref[...]).astype(BF16)
    proj = _dot(hb, win_ref[...])
    gate = proj[:, Q_LORA:]
    qnt = _rms(proj[:, :Q_LORA], qn_ref[...]).T.astype(BF16)
    qqt = _dot(wqt_ref[...], qnt)
    c = (NOPE + ROPE) ** -0.5 * 1.4426950408889634
    n0, n1 = H_B * NOPE, H_B * (NOPE + ROPE)
    q_sc[:n0, :] = (qqt[:n0] * c).astype(BF16)
    q_sc[n0:n1, :] = ((qqt[n0:n1] * cos_ref[...] + qqt[n1:] * sin_ref[...]) * c).astype(BF16)
    q_sc[n1:, :] = jnp.zeros((ROPE, tq), BF16)
    m_sc[...] = jnp.full(m_sc.shape, -jnp.inf, F32)
    acc_sc[...] = jnp.zeros(acc_sc.shape, F32)

    ones = jnp.ones((ACC_ROWS - V_B, tk), BF16)
    rk = lax.broadcasted_iota(jnp.int32, (tk, tq), 0)
    cq = lax.broadcasted_iota(jnp.int32, (tk, tq), 1)
    cshift = CHUNK.bit_length() - 1

    def step(j, diagonal):
        k0 = pl.multiple_of(j * tk, tk)
        if diagonal:
            visible = jnp.right_shift(rk, cshift) <= jnp.right_shift(cq, cshift)
        def scores(h):
            kb = jnp.concatenate(
                [kn_ref[0, pl.ds(k0, tk), h * NOPE:(h + 1) * NOPE], kr_ref[0, pl.ds(k0, tk), :]], axis=1)
            qt = jnp.concatenate(
                [q_sc[h * NOPE:(h + 1) * NOPE, :], q_sc[n0 + h * ROPE:n0 + h * ROPE + LANES, :]], axis=0)
            return _dot(kb, qt)

        for h in range(H_B):
            s_sc[h] = scores(h)
        alphas = []
        for h in range(H_B):
            s = s_sc[h]
            if diagonal:
                s = jnp.where(visible, s, -jnp.inf)
            m_prev = m_sc[h]
            m_new = jnp.maximum(m_prev, jnp.max(s, axis=0, keepdims=True))
            alphas.append(jnp.exp2(m_prev - m_new))
            p_sc[h] = jnp.exp2(s - m_new).astype(BF16)
            m_sc[h] = m_new
        for h in range(H_B):
            vt = jnp.concatenate([vt_ref[0, j, h * V_B:(h + 1) * V_B, :], ones], axis=0)
            acc_sc[h] = alphas[h] * acc_sc[h] + _dot(vt, p_sc[h])

    def body(j, carry):
        step(j, False)
        return carry

    lax.fori_loop(0, i, body, 0)
    step(i, True)

    outs = []
    for h in range(H_B):
        a = acc_sc[h]
        outs.append(a[:V_B] * (1.0 / a[V_B:V_B + 1]))
    o = jnp.concatenate(outs, axis=0).T
    og = (o * _silu(gate)).astype(BF16)
    y = _dot(og, wout_ref[...])
    y_ref[0] = x + _rms(y, npost_ref[...])


def _mla_prompt_layer(x, kn, kr0, vt, cos8t, sin8t, npre, win, qn, wqt, wout, npost, *, tq, tk):
    bsz, tlen, d = x.shape
    assert tq == tk and tq % CHUNK == 0 and vt.shape[3] == tk
    tok = pl.BlockSpec((1, tq, d), lambda b, t: (b, t, 0))
    full = lambda w: pl.BlockSpec((1, tlen, w), lambda b, t: (b, 0, 0))
    tab = pl.BlockSpec((cos8t.shape[0], tq), lambda b, t: (0, t))
    return pl.pallas_call(
        functools.partial(_mla_kernel, tq=tq, tk=tk),
        grid=(bsz, tlen // tq),
        in_specs=[tok, full(kn.shape[2]), full(kr0.shape[2]),
                  pl.BlockSpec((1,) + vt.shape[1:], lambda b, t: (b, 0, 0, 0)), tab, tab,
                  _const_spec(npre.shape), _const_spec(win.shape), _const_spec(qn.shape),
                  _const_spec(wqt.shape), _const_spec(wout.shape), _const_spec(npost.shape)],
        out_specs=tok,
        out_shape=jax.ShapeDtypeStruct((bsz, tlen, d), F32),
        scratch_shapes=[pltpu.VMEM((H_B * (NOPE + ROPE) + ROPE, tq), BF16),
                        pltpu.VMEM((H_B, 1, tq), F32),
                        pltpu.VMEM((H_B, ACC_ROWS, tq), F32),
                        pltpu.VMEM((H_B, tk, tq), F32),
                        pltpu.VMEM((H_B, tk, tq), BF16)],
        compiler_params=pltpu.CompilerParams(
            dimension_semantics=("parallel", "arbitrary"), vmem_limit_bytes=VMEM_LIMIT),
        name="mla_prompt_layer",
    )(x, kn, kr0, vt, cos8t, sin8t, npre, win, qn, wqt, wout, npost)


def _mla_sample_kernel(x_ref, cckv_ref, ckr_ref, nckv_ref, nkr_ref, cos_ref, sin_ref, npre_ref, win_ref,
                       qn_ref, wq_ref, wuk_ref, wuv_ref, wout_ref, npost_ref, y_ref, *, past):
    x = x_ref[0]
    t = x.shape[0]
    q_nope, q_rope, gate = _mla_q(x, cos_ref[...], sin_ref[...], npre_ref, win_ref, qn_ref, wq_ref, LANES)

    qlat = jnp.concatenate(
        [_dot_nt(q_nope[:, h * NOPE:(h + 1) * NOPE], wuk_ref[:, h * NOPE:(h + 1) * NOPE]) for h in range(H_B)],
        axis=0).astype(BF16)
    qr = jnp.concatenate([q_rope[:, h * LANES:h * LANES + ROPE] for h in range(H_B)], axis=0)

    c_ckv = cckv_ref[0].astype(BF16)
    n_ckv = nckv_ref[0].astype(BF16)
    s_c = _dot_nt(qlat, c_ckv) + _dot_nt(qr, ckr_ref[0].astype(BF16))
    s_n = _dot_nt(qlat, n_ckv) + _dot_nt(qr, nkr_ref[0].astype(BF16))

    cshift = CHUNK.bit_length() - 1

    def chunk_mask(s, k_start):
        rows = lax.broadcasted_iota(jnp.int32, s.shape, 0)
        cols = lax.broadcasted_iota(jnp.int32, s.shape, 1)
        q_chunk = jnp.right_shift(past + jnp.bitwise_and(rows, t - 1), cshift)
        k_chunk = jnp.right_shift(k_start + cols, cshift)
        return jnp.where(k_chunk <= q_chunk, s, -jnp.inf)

    s_c = chunk_mask(s_c, 0)
    s_n = chunk_mask(s_n, past)
    m = jnp.maximum(jnp.max(s_c, axis=-1, keepdims=True), jnp.max(s_n, axis=-1, keepdims=True))
    p_c = jnp.exp(s_c - m)
    p_n = jnp.exp(s_n - m)
    l = jnp.sum(p_c, axis=-1, keepdims=True) + jnp.sum(p_n, axis=-1, keepdims=True)
    olat = (_dot(p_c.astype(BF16), c_ckv) + _dot(p_n.astype(BF16), n_ckv)) * (1.0 / l)
    olat = olat.astype(BF16)
    o = jnp.concatenate(
        [_dot(olat[h * t:(h + 1) * t], wuv_ref[:, h * V_B:(h + 1) * V_B]) for h in range(H_B)], axis=1)
    og = (o * _silu(gate)).astype(BF16)
    y = _dot(og, wout_ref[...])
    y_ref[0] = x + _rms(y, npost_ref[...])


def _mla_sample_layer(x, cache_ckv, cache_kr, new_ckv, new_kr, cos, sin, npre, win, qn, wq, wuk, wuv,
                      wout, npost):
    bsz, t, d = x.shape
    past = cache_ckv.shape[1]
    assert t & (t - 1) == 0
    per_b = lambda a: pl.BlockSpec((1,) + a.shape[1:], lambda b: (b, 0, 0))
    consts = [cos, sin, npre, win, qn, wq, wuk, wuv, wout, npost]
    return pl.pallas_call(
        functools.partial(_mla_sample_kernel, past=past),
        grid=(bsz,),
        in_specs=[per_b(x), per_b(cache_ckv), per_b(cache_kr), per_b(new_ckv), per_b(new_kr)]
                 + [_const_spec(a.shape) for a in consts],
        out_specs=per_b(x),
        out_shape=jax.ShapeDtypeStruct((bsz, t, d), F32),
        compiler_params=pltpu.CompilerParams(
            dimension_semantics=("parallel",), vmem_limit_bytes=VMEM_LIMIT),
        name="mla_sample_layer",
    )(x, cache_ckv, cache_kr, new_ckv, new_kr, *consts)


def _rope_tables(pos_offset, tlen):
    inv = jnp.power(ROPE_BASE, -jnp.arange(HALF, dtype=F32) / HALF)
    ang = (pos_offset + jnp.arange(tlen)).astype(F32)[:, None] * inv[None, :]
    cos = jnp.cos(ang)
    sin = jnp.sin(ang)
    return jnp.concatenate([cos, cos], axis=1), jnp.concatenate([sin, sin], axis=1)


def _rot_cols(w):
    return jnp.concatenate([-w[..., HALF:], w[..., :HALF]], axis=-1)


def _row(a):
    return a.reshape(1, -1)


def kernel(x_prompt, x_sample, state_gla, cache_ckv, cache_krope, a_norm_pre, a_w_in, a_w_gate2, a_b_gate,
           a_head_norm, a_w_out, a_norm_post, kv_norm, kv_w_down, kv_latent_norm, kv_w_uk, kv_w_uv,
           b_norm_pre, b_w_in, b_q_norm, b_w_uq, b_w_out, b_norm_post):
    past = cache_ckv.shape[1]
    t_p, t_s = x_prompt.shape[1], x_sample.shape[1]

    a_win = a_w_in.astype(BF16)
    a_wg2 = a_w_gate2.astype(BF16)
    a_wout = a_w_out.astype(BF16)
    wdc = kv_w_down[:, :KV_LORA].astype(BF16)
    wdr_cols = kv_w_down[:, KV_LORA:]
    wdr = jnp.concatenate([wdr_cols, wdr_cols, _rot_cols(wdr_cols), _rot_cols(wdr_cols)], axis=1).astype(BF16)
    wuk = kv_w_uk.astype(BF16)
    wuv = kv_w_uv.astype(BF16)
    b_win = b_w_in.astype(BF16)
    b_wout = b_w_out.astype(BF16)
    uq = b_w_uq.reshape(N_B, Q_LORA, H_B, NOPE + ROPE)
    uq_nope = uq[..., :NOPE].reshape(N_B, Q_LORA, H_B * NOPE)
    uq_rope = uq[..., NOPE:]
    uq_rot = _rot_cols(uq_rope)
    wqt_p = jnp.concatenate([uq_nope, uq_rope.reshape(N_B, Q_LORA, H_B * ROPE),
                             uq_rot.reshape(N_B, Q_LORA, H_B * ROPE)], axis=-1).astype(BF16).transpose(0, 2, 1)
    wuvt = wuv.T
    pad = lambda w: jnp.pad(w, ((0, 0), (0, 0), (0, 0), (0, LANES - ROPE))).reshape(N_B, Q_LORA, H_B * LANES)
    wq_s = jnp.concatenate([uq_nope, pad(uq_rope), pad(uq_rot)], axis=-1).astype(BF16)

    cos_p, sin_p = _rope_tables(0, t_p)
    cos_s, sin_s = _rope_tables(past, t_s)
    tile2 = lambda a: jnp.concatenate([a, a], axis=1)
    cos8t_p, sin8t_p = jnp.tile(cos_p.T, (H_B, 1)), jnp.tile(sin_p.T, (H_B, 1))
    padl = lambda a: jnp.tile(jnp.pad(a, ((0, 0), (0, LANES - ROPE))), (1, H_B))
    cos8_s, sin8_s = padl(cos_s), padl(sin_s)

    def gla_stack(x, states, rows, chunk):
        outs = []
        for l in range(N_A):
            s0 = None if states is None else states[l]
            x, s_new = _gla_layer(x, s0, _row(a_norm_pre[l]), a_win[l], a_wg2[l], _row(a_b_gate[l]),
                                  _row(a_head_norm[l]), a_wout[l], _row(a_norm_post[l]),
                                  rows=rows, chunk=chunk)
            outs.append(s_new)
        return x, jnp.stack(outs)

    x, st_p = gla_stack(x_prompt, None, GLA_ROWS, CHUNK)
    ckv_p, kr_p, kn, kr0, vt = _shared_kv(x, tile2(cos_p), tile2(sin_p), _row(kv_norm), wdc, wdr,
                                          _row(kv_latent_norm), wuk, wuvt, rows=KV_ROWS, kblk=ATT_K)
    for j in range(N_B):
        x = _mla_prompt_layer(x, kn, kr0, vt, cos8t_p, sin8t_p, _row(b_norm_pre[j]), b_win[j],
                              _row(b_q_norm[j]), wqt_p[j], b_wout[j], _row(b_norm_post[j]),
                              tq=ATT_Q, tk=ATT_K)
    y_prompt = x

    x, st_s = gla_stack(x_sample, state_gla, t_s, min(CHUNK, t_s))
    ckv_s, kr_s = _shared_kv(x, tile2(cos_s), tile2(sin_s), _row(kv_norm), wdc, wdr,
                             _row(kv_latent_norm), rows=t_s)
    for j in range(N_B):
        x = _mla_sample_layer(x, cache_ckv, cache_krope, ckv_s, kr_s, cos8_s, sin8_s,
                              _row(b_norm_pre[j]), b_win[j], _row(b_q_norm[j]), wq_s[j], wuk, wuv,
                              b_wout[j], _row(b_norm_post[j]))
    y_sample = x

    return (y_prompt, y_sample, st_p, st_s, ckv_p, kr_p, ckv_s, kr_s)
```

```python
import functools

import jax
import jax.numpy as jnp
from jax import lax
from jax.experimental import pallas as pl
from jax.experimental.pallas import tpu as pltpu

F32 = jnp.float32
BF16 = jnp.bfloat16

D_MODEL = 1024
CHUNK = 64
N_A = 2
N_B = 2
H_A = 4
DK_A = 128
DV_A = 256
KD_A = H_A * DK_A
VD_A = H_A * DV_A
GATE_RANK = 16
GATE_TAU = 16.0
H_B = 8
NOPE = 128
ROPE = 64
HALF = ROPE // 2
V_B = 128
VD_B = H_B * V_B
Q_LORA = 384
KV_LORA = 256
ROPE_BASE = 10000.0
EPS = 1e-6

LANES = 128
VMEM_LIMIT = 56 * 1024 * 1024

GLA_ROWS = 256
KV_ROWS = 512
ATT_Q = 512
ATT_K = 256
MASKED = -1e30


def _dot(a, b):
    return jnp.dot(a, b, preferred_element_type=F32)


def _dot_nt(a, b):
    return lax.dot_general(a, b, (((1,), (1,)), ((), ())), preferred_element_type=F32)


def _dot_tn(a, b):
    return lax.dot_general(a, b, (((0,), (0,)), ((), ())), preferred_element_type=F32)


def _rms(xf, g):
    ms = jnp.mean(xf * xf, axis=-1, keepdims=True)
    return xf * lax.rsqrt(ms + EPS) * g


def _silu(g):
    return g * (1.0 / (1.0 + jnp.exp(-g)))


def _const_spec(shape):
    return pl.BlockSpec(shape, lambda *_: (0,) * len(shape))


def _gla_kernel(*refs, rows, chunk, has_state):
    if has_state:
        (x_ref, s0_ref, npre_ref, win_ref, wg2_ref, bg_ref, hn_ref, wout_ref, npost_ref,
         y_ref, sout_ref, st_sc, o_sc, qe_sc, u_sc, stb_sc) = refs
    else:
        (x_ref, npre_ref, win_ref, wg2_ref, bg_ref, hn_ref, wout_ref, npost_ref,
         y_ref, sout_ref, st_sc, o_sc, qe_sc, u_sc, stb_sc) = refs
    t = pl.program_id(1)

    @pl.when(t == 0)
    def _():
        for h in range(H_A):
            if has_state:
                st_sc[h] = s0_ref[0, h].T
            else:
                st_sc[h] = jnp.zeros((DV_A, DK_A), F32)

    x = x_ref[0]
    hb = _rms(x, npre_ref[...]).astype(BF16)
    n_main = 2 * KD_A + 2 * VD_A
    glr = _dot(hb, win_ref[:, n_main:]).astype(BF16)
    z = _dot(glr, wg2_ref[...]) + bg_ref[...]
    g = (jnp.minimum(z, 0.0) - jnp.log(1.0 + jnp.exp(-jnp.abs(z)))) * (1.0 / GATE_TAU)

    shift = chunk.bit_length() - 1
    ri = lax.broadcasted_iota(jnp.int32, (rows, rows), 0)
    ci = lax.broadcasted_iota(jnp.int32, (rows, rows), 1)
    inchunk = jnp.where(jnp.right_shift(ri, shift) == jnp.right_shift(ci, shift), ci, rows) <= ri
    tri = jnp.where(inchunk, 1.0, 0.0).astype(BF16)
    g_hi = g.astype(BF16)
    g_lo = (g - g_hi.astype(F32)).astype(BF16)
    b = _dot(tri, g_hi) + _dot(tri, g_lo)

    proj = _dot(hb, win_ref[:, :n_main])
    scale = DK_A ** -0.5
    nc = rows // chunk
    decay = [[None] * nc for _ in range(H_A)]

    for h in range(H_A):
        bh = b[:, h * DK_A:(h + 1) * DK_A]
        ends = [bh[(c + 1) * chunk - 1:(c + 1) * chunk, :] for c in range(nc)]
        bl = jnp.concatenate([jnp.broadcast_to(e, (chunk, DK_A)) for e in ends], axis=0)
        qh = proj[:, h * DK_A:(h + 1) * DK_A]
        kh = proj[:, KD_A + h * DK_A:KD_A + (h + 1) * DK_A]
        vh = proj[:, 2 * KD_A + h * DV_A:2 * KD_A + (h + 1) * DV_A].astype(BF16)
        qe = (qh * scale * jnp.exp(bh)).astype(BF16)
        ke = (kh * jnp.exp(-bh)).astype(BF16)
        kd = (kh * jnp.exp(bl - bh)).astype(BF16)
        s = jnp.where(inchunk, _dot_nt(qe, ke), 0.0).astype(BF16)
        o_sc[:, h * DV_A:(h + 1) * DV_A] = _dot(s, vh)
        qe_sc[:, h * DK_A:(h + 1) * DK_A] = qe
        for c in range(nc):
            r0, r1 = c * chunk, (c + 1) * chunk
            u_sc[h, c] = _dot_tn(vh[r0:r1], kd[r0:r1])
            decay[h][c] = jnp.exp(ends[c])

    for h in range(H_A):
        st = st_sc[h]
        for c in range(nc):
            stb_sc[h, c] = st.astype(BF16)
            st = st * decay[h][c] + u_sc[h, c]
        st_sc[h] = st

    for c in range(nc):
        r0, r1 = c * chunk, (c + 1) * chunk
        for h in range(H_A):
            o_sc[r0:r1, h * DV_A:(h + 1) * DV_A] += _dot_nt(qe_sc[r0:r1, h * DK_A:(h + 1) * DK_A], stb_sc[h, c])

    gate = proj[:, 2 * KD_A + VD_A:2 * KD_A + 2 * VD_A]
    heads = []
    for h in range(H_A):
        heads.append(_rms(o_sc[:, h * DV_A:(h + 1) * DV_A], hn_ref[:, h * DV_A:(h + 1) * DV_A]))
    og = (jnp.concatenate(heads, axis=1) * _silu(gate)).astype(BF16)
    y = _dot(og, wout_ref[...])
    y_ref[0] = x + _rms(y, npost_ref[...])

    @pl.when(t == pl.num_programs(1) - 1)
    def _():
        for h in range(H_A):
            sout_ref[0, h] = st_sc[h].T


def _gla_layer(x, s0, npre, win, wg2, bg, hn, wout, npost, *, rows, chunk):
    bsz, tlen, d = x.shape
    has_state = s0 is not None
    tok = pl.BlockSpec((1, rows, d), lambda b, t: (b, t, 0))
    st_spec = pl.BlockSpec((1, H_A, DK_A, DV_A), lambda b, t: (b, 0, 0, 0))
    in_specs = [tok] + ([st_spec] if has_state else []) + [
        _const_spec(npre.shape), _const_spec(win.shape), _const_spec(wg2.shape), _const_spec(bg.shape),
        _const_spec(hn.shape), _const_spec(wout.shape), _const_spec(npost.shape)]
    args = [x] + ([s0] if has_state else []) + [npre, win, wg2, bg, hn, wout, npost]
    return pl.pallas_call(
        functools.partial(_gla_kernel, rows=rows, chunk=chunk, has_state=has_state),
        grid=(bsz, tlen // rows),
        in_specs=in_specs,
        out_specs=(tok, st_spec),
        out_shape=(jax.ShapeDtypeStruct((bsz, tlen, d), F32),
                   jax.ShapeDtypeStruct((bsz, H_A, DK_A, DV_A), F32)),
        scratch_shapes=[pltpu.VMEM((H_A, DV_A, DK_A), F32), pltpu.VMEM((rows, VD_A), F32),
                        pltpu.VMEM((rows, KD_A), BF16),
                        pltpu.VMEM((H_A, rows // chunk, DV_A, DK_A), F32),
                        pltpu.VMEM((H_A, rows // chunk, DV_A, DK_A), BF16)],
        compiler_params=pltpu.CompilerParams(
            dimension_semantics=("parallel", "arbitrary"), vmem_limit_bytes=VMEM_LIMIT),
        name="gla_layer",
    )(*args)


def _kv_kernel(x_ref, cos_ref, sin_ref, nkv_ref, wdc_ref, wdr_ref, nlat_ref, *rest, expand, kblk):
    if expand:
        wuk_ref, wuvt_ref, koh_ref, ckv_ref, kr_ref, kn_ref, kr0_ref, vt_ref = rest
    else:
        ckv_ref, kr_ref = rest
    hb = _rms(x_ref[0], nkv_ref[...]).astype(BF16)
    ckv = _rms(_dot(hb, wdc_ref[...]), nlat_ref[...])
    ckv_ref[0] = ckv
    rr = _dot(hb, wdr_ref[...])
    kr2 = rr[:, :LANES] * cos_ref[...] + rr[:, LANES:] * sin_ref[...]
    kr_ref[0] = kr2[:, :ROPE]
    if expand:
        cb = ckv.astype(BF16)
        kn_ref[0] = _dot(cb, wuk_ref[...]).astype(BF16)
        lane = lax.broadcasted_iota(jnp.int32, kr2.shape, 1)
        kr0_ref[0] = jnp.where(lane < ROPE, kr2, koh_ref[...]).astype(BF16)
        vt = _dot_nt(wuvt_ref[...], cb).astype(BF16)
        for i in range(vt.shape[1] // kblk):
            vt_ref[0, i] = vt[:, i * kblk:(i + 1) * kblk]


def _shared_kv(x, cos2, sin2, nkv, wdc, wdr, nlat, wuk=None, wuvt=None, koh=None, *, rows, kblk=None):
    bsz, tlen, d = x.shape
    expand = wuk is not None
    tok = lambda w: pl.BlockSpec((1, rows, w), lambda b, t: (b, t, 0))
    tab = pl.BlockSpec((rows, LANES), lambda b, t: (t, 0))
    in_specs = [tok(d), tab, tab, _const_spec(nkv.shape), _const_spec(wdc.shape),
                _const_spec(wdr.shape), _const_spec(nlat.shape)]
    args = [x, cos2, sin2, nkv, wdc, wdr, nlat]
    out_specs = [tok(KV_LORA), tok(ROPE)]
    out_shape = [jax.ShapeDtypeStruct((bsz, tlen, KV_LORA), F32),
                 jax.ShapeDtypeStruct((bsz, tlen, ROPE), F32)]
    if expand:
        in_specs += [_const_spec(wuk.shape), _const_spec(wuvt.shape), tab]
        args += [wuk, wuvt, koh]
        out_specs += [tok(H_B * NOPE), tok(LANES),
                      pl.BlockSpec((1, rows // kblk, VD_B, kblk), lambda b, t: (b, t, 0, 0))]
        out_shape += [jax.ShapeDtypeStruct((bsz, tlen, H_B * NOPE), BF16),
                      jax.ShapeDtypeStruct((bsz, tlen, LANES), BF16),
                      jax.ShapeDtypeStruct((bsz, tlen // kblk, VD_B, kblk), BF16)]
    return pl.pallas_call(
        functools.partial(_kv_kernel, expand=expand, kblk=kblk),
        grid=(bsz, tlen // rows),
        in_specs=in_specs,
        out_specs=tuple(out_specs),
        out_shape=tuple(out_shape),
        compiler_params=pltpu.CompilerParams(
            dimension_semantics=("parallel", "parallel"), vmem_limit_bytes=VMEM_LIMIT),
        name="shared_kv",
    )(*args)


def _mla_q(x, cos, sin, npre_ref, win_ref, qn_ref, wq_ref, rope_w):
    hb = _rms(x, npre_ref[...]).astype(BF16)
    proj = _dot(hb, win_ref[...])
    qn = _rms(proj[:, :Q_LORA], qn_ref[...]).astype(BF16)
    qq = _dot(qn, wq_ref[...])
    scale = (NOPE + ROPE) ** -0.5
    n0, n1 = H_B * NOPE, H_B * NOPE + H_B * rope_w
    q_nope = (qq[:, :n0] * scale).astype(BF16)
    q_rope = ((qq[:, n0:n1] * cos + qq[:, n1:] * sin) * scale).astype(BF16)
    return q_nope, q_rope, proj[:, Q_LORA:]


ACC_ROWS = V_B + 16


def _mla_kernel(x_ref, kn_ref, kr_ref, vt_ref, cos_ref, sin_ref, npre_ref, win_ref, qn_ref, wqt_ref,
                wout_ref, npost_ref, y_ref, q_sc, m_sc, acc_sc, alpha_sc, p_sc, *, tq, tk):
    i = pl.program_id(1)
    x = x_ref[0]
    hb = _rms(x, npre_ref[...]).astype(BF16)
    qlat = _dot(hb, win_ref[:, :Q_LORA])
    qnt = _rms(qlat, qn_ref[...]).T.astype(BF16)
    qqt = _dot(wqt_ref[...], qnt)
    c = (NOPE + ROPE) ** -0.5 * 1.4426950408889634
    n0, n1 = H_B * NOPE, H_B * (NOPE + ROPE)
    q_sc[:n0, :] = (qqt[:n0] * c).astype(BF16)
    cos, sin = cos_ref[...], sin_ref[...]
    for h in range(H_B):
        r0 = n0 + h * ROPE
        x1, x2 = qqt[r0:r0 + HALF], qqt[r0 + HALF:r0 + ROPE]
        q_sc[r0:r0 + HALF, :] = ((x1 * cos - x2 * sin) * c).astype(BF16)
        q_sc[r0 + HALF:r0 + ROPE, :] = ((x2 * cos + x1 * sin) * c).astype(BF16)
    row = lax.broadcasted_iota(jnp.int32, (ROPE, tq), 0)
    q_chunk = jnp.right_shift(i * tq + lax.broadcasted_iota(jnp.int32, (ROPE, tq), 1), CHUNK.bit_length() - 1)
    q_sc[n1:, :] = jnp.where(row > q_chunk, MASKED, 0.0).astype(BF16)
    m_sc[...] = jnp.full(m_sc.shape, -jnp.inf, F32)
    acc_sc[...] = jnp.zeros(acc_sc.shape, F32)

    ones = jnp.ones((ACC_ROWS - V_B, tk), BF16)

    def scores(j, h):
        k0 = pl.multiple_of(j * tk, tk)
        kb = jnp.concatenate(
            [kn_ref[0, pl.ds(k0, tk), h * NOPE:(h + 1) * NOPE], kr_ref[0, pl.ds(k0, tk), :]], axis=1)
        qt = jnp.concatenate(
            [q_sc[h * NOPE:(h + 1) * NOPE, :], q_sc[n0 + h * ROPE:n0 + (h + 1) * ROPE, :], q_sc[n1:, :]], axis=0)
        return _dot(kb, qt)

    def softmax(s, h, slot):
        m_prev = m_sc[h]
        m_new = jnp.maximum(m_prev, jnp.max(s, axis=0, keepdims=True))
        alpha_sc[slot, h] = jnp.exp2(m_prev - m_new)
        p_sc[slot, h] = jnp.exp2(s - m_new).astype(BF16)
        m_sc[h] = m_new

    def values(j, h, slot):
        vt = jnp.concatenate([vt_ref[0, j, h * V_B:(h + 1) * V_B, :], ones], axis=0)
        acc_sc[h] = alpha_sc[slot, h] * acc_sc[h] + _dot(vt, p_sc[slot, h])

    for h in range(H_B):
        softmax(scores(0, h), h, 0)

    def block(j, slot_new, slot_old):
        s = [scores(j, h) for h in range(H_B)]
        for h in range(H_B):
            values(j - 1, h, slot_old)
        for h in range(H_B):
            softmax(s[h], h, slot_new)

    def pair(jj, carry):
        block(2 * jj + 1, 1, 0)
        block(2 * jj + 2, 0, 1)
        return carry

    def tail_odd(last):
        block(last, 1, 0)
        for h in range(H_B):
            values(last, h, 1)

    def tail_even(last):
        for h in range(H_B):
            values(last, h, 0)

    if tq == 2 * tk:
        lax.fori_loop(0, i, pair, 0)
        tail_odd(2 * i + 1)
    else:
        lax.fori_loop(0, i // 2, pair, 0)
        odd = lax.rem(i, 2) == 1
        pl.when(odd)(lambda: tail_odd(i))
        pl.when(jnp.logical_not(odd))(lambda: tail_even(i))

    gate = _dot(hb, win_ref[:, Q_LORA:])
    outs = []
    for h in range(H_B):
        a = acc_sc[h]
        outs.append(a[:V_B] * (1.0 / a[V_B:V_B + 1]))
    o = jnp.concatenate(outs, axis=0).T
    og = (o * _silu(gate)).astype(BF16)
    y = _dot(og, wout_ref[...])
    y_ref[0] = x + _rms(y, npost_ref[...])


def _mla_prompt_layer(x, kn, kr0, vt, cos8t, sin8t, npre, win, qn, wqt, wout, npost, *, tq, tk):
    bsz, tlen, d = x.shape
    assert tq in (tk, 2 * tk) and tk % CHUNK == 0 and vt.shape[3] == tk
    tok = pl.BlockSpec((1, tq, d), lambda b, t: (b, t, 0))
    full = lambda w: pl.BlockSpec((1, tlen, w), lambda b, t: (b, 0, 0))
    tab = pl.BlockSpec((cos8t.shape[0], tq), lambda b, t: (0, t))
    return pl.pallas_call(
        functools.partial(_mla_kernel, tq=tq, tk=tk),
        grid=(bsz, tlen // tq),
        in_specs=[tok, full(kn.shape[2]), full(kr0.shape[2]),
                  pl.BlockSpec((1,) + vt.shape[1:], lambda b, t: (b, 0, 0, 0)), tab, tab,
                  _const_spec(npre.shape), _const_spec(win.shape), _const_spec(qn.shape),
                  _const_spec(wqt.shape), _const_spec(wout.shape), _const_spec(npost.shape)],
        out_specs=tok,
        out_shape=jax.ShapeDtypeStruct((bsz, tlen, d), F32),
        scratch_shapes=[pltpu.VMEM((H_B * (NOPE + ROPE) + ROPE, tq), BF16),
                        pltpu.VMEM((H_B, 1, tq), F32),
                        pltpu.VMEM((H_B, ACC_ROWS, tq), F32),
                        pltpu.VMEM((2, H_B, 1, tq), F32),
                        pltpu.VMEM((2, H_B, tk, tq), BF16)],
        compiler_params=pltpu.CompilerParams(
            dimension_semantics=("parallel", "arbitrary"), vmem_limit_bytes=VMEM_LIMIT),
        name="mla_prompt_layer",
    )(x, kn, kr0, vt, cos8t, sin8t, npre, win, qn, wqt, wout, npost)


def _mla_sample_kernel(x_ref, cckv_ref, ckr_ref, nckv_ref, nkr_ref, cos_ref, sin_ref, npre_ref, win_ref,
                       qn_ref, wq_ref, wuk_ref, wuv_ref, wout_ref, npost_ref, y_ref, *, past):
    x = x_ref[0]
    t = x.shape[0]
    q_nope, q_rope, gate = _mla_q(x, cos_ref[...], sin_ref[...], npre_ref, win_ref, qn_ref, wq_ref, LANES)

    qlat = jnp.concatenate(
        [_dot_nt(q_nope[:, h * NOPE:(h + 1) * NOPE], wuk_ref[:, h * NOPE:(h + 1) * NOPE]) for h in range(H_B)],
        axis=0).astype(BF16)
    qr = jnp.concatenate([q_rope[:, h * LANES:h * LANES + ROPE] for h in range(H_B)], axis=0)

    c_ckv = cckv_ref[0].astype(BF16)
    n_ckv = nckv_ref[0].astype(BF16)
    s_c = _dot_nt(qlat, c_ckv) + _dot_nt(qr, ckr_ref[0].astype(BF16))
    s_n = _dot_nt(qlat, n_ckv) + _dot_nt(qr, nkr_ref[0].astype(BF16))

    cshift = CHUNK.bit_length() - 1

    def chunk_mask(s, k_start):
        rows = lax.broadcasted_iota(jnp.int32, s.shape, 0)
        cols = lax.broadcasted_iota(jnp.int32, s.shape, 1)
        q_chunk = jnp.right_shift(past + jnp.bitwise_and(rows, t - 1), cshift)
        k_chunk = jnp.right_shift(k_start + cols, cshift)
        return jnp.where(k_chunk <= q_chunk, s, -jnp.inf)

    s_c = chunk_mask(s_c, 0)
    s_n = chunk_mask(s_n, past)
    m = jnp.maximum(jnp.max(s_c, axis=-1, keepdims=True), jnp.max(s_n, axis=-1, keepdims=True))
    p_c = jnp.exp(s_c - m)
    p_n = jnp.exp(s_n - m)
    l = jnp.sum(p_c, axis=-1, keepdims=True) + jnp.sum(p_n, axis=-1, keepdims=True)
    olat = (_dot(p_c.astype(BF16), c_ckv) + _dot(p_n.astype(BF16), n_ckv)) * (1.0 / l)
    olat = olat.astype(BF16)
    o = jnp.concatenate(
        [_dot(olat[h * t:(h + 1) * t], wuv_ref[:, h * V_B:(h + 1) * V_B]) for h in range(H_B)], axis=1)
    og = (o * _silu(gate)).astype(BF16)
    y = _dot(og, wout_ref[...])
    y_ref[0] = x + _rms(y, npost_ref[...])


def _mla_sample_layer(x, cache_ckv, cache_kr, new_ckv, new_kr, cos, sin, npre, win, qn, wq, wuk, wuv,
                      wout, npost):
    bsz, t, d = x.shape
    past = cache_ckv.shape[1]
    assert t & (t - 1) == 0
    per_b = lambda a: pl.BlockSpec((1,) + a.shape[1:], lambda b: (b, 0, 0))
    consts = [cos, sin, npre, win, qn, wq, wuk, wuv, wout, npost]
    return pl.pallas_call(
        functools.partial(_mla_sample_kernel, past=past),
        grid=(bsz,),
        in_specs=[per_b(x), per_b(cache_ckv), per_b(cache_kr), per_b(new_ckv), per_b(new_kr)]
                 + [_const_spec(a.shape) for a in consts],
        out_specs=per_b(x),
        out_shape=jax.ShapeDtypeStruct((bsz, t, d), F32),
        compiler_params=pltpu.CompilerParams(
            dimension_semantics=("parallel",), vmem_limit_bytes=VMEM_LIMIT),
        name="mla_sample_layer",
    )(x, cache_ckv, cache_kr, new_ckv, new_kr, *consts)


def _rope_tables(pos_offset, tlen):
    inv = jnp.power(ROPE_BASE, -jnp.arange(HALF, dtype=F32) / HALF)
    ang = (pos_offset + jnp.arange(tlen)).astype(F32)[:, None] * inv[None, :]
    cos = jnp.cos(ang)
    sin = jnp.sin(ang)
    return jnp.concatenate([cos, cos], axis=1), jnp.concatenate([sin, sin], axis=1)


def _rot_cols(w):
    return jnp.concatenate([-w[..., HALF:], w[..., :HALF]], axis=-1)


def _row(a):
    return a.reshape(1, -1)


def kernel(x_prompt, x_sample, state_gla, cache_ckv, cache_krope, a_norm_pre, a_w_in, a_w_gate2, a_b_gate,
           a_head_norm, a_w_out, a_norm_post, kv_norm, kv_w_down, kv_latent_norm, kv_w_uk, kv_w_uv,
           b_norm_pre, b_w_in, b_q_norm, b_w_uq, b_w_out, b_norm_post):
    past = cache_ckv.shape[1]
    t_p, t_s = x_prompt.shape[1], x_sample.shape[1]

    a_win = a_w_in.astype(BF16)
    a_wg2 = a_w_gate2.astype(BF16)
    a_wout = a_w_out.astype(BF16)
    wdc = kv_w_down[:, :KV_LORA].astype(BF16)
    wdr_cols = kv_w_down[:, KV_LORA:]
    wdr = jnp.concatenate([wdr_cols, wdr_cols, _rot_cols(wdr_cols), _rot_cols(wdr_cols)], axis=1).astype(BF16)
    wuk = kv_w_uk.astype(BF16)
    wuv = kv_w_uv.astype(BF16)
    b_win = b_w_in.astype(BF16)
    b_wout = b_w_out.astype(BF16)
    uq = b_w_uq.reshape(N_B, Q_LORA, H_B, NOPE + ROPE)
    uq_nope = uq[..., :NOPE].reshape(N_B, Q_LORA, H_B * NOPE)
    uq_rope = uq[..., NOPE:]
    uq_rot = _rot_cols(uq_rope)
    wqt_p = jnp.concatenate([uq_nope, uq_rope.reshape(N_B, Q_LORA, H_B * ROPE)],
                            axis=-1).astype(BF16).transpose(0, 2, 1)
    wuvt = wuv.T
    pad = lambda w: jnp.pad(w, ((0, 0), (0, 0), (0, 0), (0, LANES - ROPE))).reshape(N_B, Q_LORA, H_B * LANES)
    wq_s = jnp.concatenate([uq_nope, pad(uq_rope), pad(uq_rot)], axis=-1).astype(BF16)

    cos_p, sin_p = _rope_tables(0, t_p)
    cos_s, sin_s = _rope_tables(past, t_s)
    tile2 = lambda a: jnp.concatenate([a, a], axis=1)
    cos8t_p, sin8t_p = cos_p[:, :HALF].T, sin_p[:, :HALF].T
    padl = lambda a: jnp.tile(jnp.pad(a, ((0, 0), (0, LANES - ROPE))), (1, H_B))
    cos8_s, sin8_s = padl(cos_s), padl(sin_s)

    def gla_stack(x, states, rows, chunk):
        outs = []
        for l in range(N_A):
            s0 = None if states is None else states[l]
            x, s_new = _gla_layer(x, s0, _row(a_norm_pre[l]), a_win[l], a_wg2[l], _row(a_b_gate[l]),
                                  _row(a_head_norm[l]), a_wout[l], _row(a_norm_post[l]),
                                  rows=rows, chunk=chunk)
            outs.append(s_new)
        return x, jnp.stack(outs)

    x, st_p = gla_stack(x_prompt, None, GLA_ROWS, CHUNK)
    assert t_p // CHUNK <= LANES - ROPE
    koh = jax.nn.one_hot(ROPE + jnp.arange(t_p) // CHUNK, LANES, dtype=F32)
    ckv_p, kr_p, kn, kr0, vt = _shared_kv(x, tile2(cos_p), tile2(sin_p), _row(kv_norm), wdc, wdr,
                                          _row(kv_latent_norm), wuk, wuvt, koh, rows=KV_ROWS, kblk=ATT_K)
    for j in range(N_B):
        x = _mla_prompt_layer(x, kn, kr0, vt, cos8t_p, sin8t_p, _row(b_norm_pre[j]), b_win[j],
                              _row(b_q_norm[j]), wqt_p[j], b_wout[j], _row(b_norm_post[j]),
                              tq=ATT_Q, tk=ATT_K)
    y_prompt = x

    x, st_s = gla_stack(x_sample, state_gla, t_s, min(CHUNK, t_s))
    ckv_s, kr_s = _shared_kv(x, tile2(cos_s), tile2(sin_s), _row(kv_norm), wdc, wdr,
                             _row(kv_latent_norm), rows=t_s)
    for j in range(N_B):
        x = _mla_sample_layer(x, cache_ckv, cache_krope, ckv_s, kr_s, cos8_s, sin8_s,
                              _row(b_norm_pre[j]), b_win[j], _row(b_q_norm[j]), wq_s[j], wuk, wuv,
                              b_wout[j], _row(b_norm_post[j]))
    y_sample = x

    return (y_prompt, y_sample, st_p, st_s, ckv_p, kr_p, ckv_s, kr_s)
```

```python
import functools

import jax
import jax.numpy as jnp
from jax import lax
from jax.experimental import pallas as pl
from jax.experimental.pallas import tpu as pltpu

F32 = jnp.float32
BF16 = jnp.bfloat16

D_MODEL = 1024
CHUNK = 64
N_A = 2
N_B = 2
H_A = 4
DK_A = 128
DV_A = 256
KD_A = H_A * DK_A
VD_A = H_A * DV_A
GATE_RANK = 16
GATE_TAU = 16.0
H_B = 8
NOPE = 128
ROPE = 64
HALF = ROPE // 2
V_B = 128
VD_B = H_B * V_B
Q_LORA = 384
KV_LORA = 256
ROPE_BASE = 10000.0
EPS = 1e-6

LANES = 128
VMEM_LIMIT = 56 * 1024 * 1024

GLA_ROWS = 512
GLA_SUB = 256
KV_ROWS = 512
ATT_Q = 512
ATT_K = 256
MASKED = -1e30


def _dot(a, b):
    return jnp.dot(a, b, preferred_element_type=F32)


def _dot_nt(a, b):
    return lax.dot_general(a, b, (((1,), (1,)), ((), ())), preferred_element_type=F32)


def _dot_tn(a, b):
    return lax.dot_general(a, b, (((0,), (0,)), ((), ())), preferred_element_type=F32)


def _rms(xf, g):
    ms = jnp.mean(xf * xf, axis=-1, keepdims=True)
    return xf * lax.rsqrt(ms + EPS) * g


def _silu(g):
    return g * (1.0 / (1.0 + jnp.exp(-g)))


def _const_spec(shape):
    return pl.BlockSpec(shape, lambda *_: (0,) * len(shape))


def _gla_kernel(*refs, rows, chunk, has_state):
    if has_state:
        (x_ref, s0_ref, npre_ref, win_ref, wglr_ref, wg2_ref, bg_ref, hn_ref, wout_ref, npost_ref,
         y_ref, sout_ref, st_sc, o_sc, qe_sc, u_sc, stb_sc) = refs
    else:
        (x_ref, npre_ref, win_ref, wglr_ref, wg2_ref, bg_ref, hn_ref, wout_ref, npost_ref,
         y_ref, sout_ref, st_sc, o_sc, qe_sc, u_sc, stb_sc) = refs
    t = pl.program_id(1)

    @pl.when(t == 0)
    def _():
        for h in range(H_A):
            if has_state:
                st_sc[h] = s0_ref[0, h]
            else:
                st_sc[h] = jnp.zeros((DK_A, DV_A), F32)

    shift = chunk.bit_length() - 1
    sub = min(rows, GLA_SUB)
    blocks = range(0, rows, sub)
    xs, z, proj, b = {}, {}, {}, {}
    for r in blocks:
        xs[r] = x_ref[0, r:r + sub, :]
        hb = _rms(xs[r], npre_ref[...]).astype(BF16)
        glr = _dot(hb, wglr_ref[...])[:, :GATE_RANK].astype(BF16)
        z[r] = _dot(glr, wg2_ref[...]) + bg_ref[...]
        proj[r] = _dot(hb, win_ref[...])

    ri = lax.broadcasted_iota(jnp.int32, (sub, sub), 0)
    ci = lax.broadcasted_iota(jnp.int32, (sub, sub), 1)
    inchunk = jnp.where(jnp.right_shift(ri, shift) == jnp.right_shift(ci, shift), ci, sub) <= ri
    tri = jnp.where(inchunk, 1.0, 0.0).astype(BF16)
    for r in blocks:
        g = (jnp.minimum(z[r], 0.0) - jnp.log(1.0 + jnp.exp(-jnp.abs(z[r])))) * (1.0 / GATE_TAU)
        g_hi = g.astype(BF16)
        g_lo = (g - g_hi.astype(F32)).astype(BF16)
        b[r] = _dot(tri, g_hi) + _dot(tri, g_lo)

    scale = DK_A ** -0.5
    nc = rows // chunk
    decay = [[None] * nc for _ in range(H_A)]

    for h in range(H_A):
        for r in blocks:
            chunks = range(r // chunk, (r + sub) // chunk)
            bh = b[r][:, h * DK_A:(h + 1) * DK_A]
            ends = {c: bh[(c + 1) * chunk - 1 - r:(c + 1) * chunk - r, :] for c in chunks}
            bl = jnp.concatenate([jnp.broadcast_to(ends[c], (chunk, DK_A)) for c in chunks], axis=0)
            qh = proj[r][:, h * DK_A:(h + 1) * DK_A]
            kh = proj[r][:, KD_A + h * DK_A:KD_A + (h + 1) * DK_A]
            vh = proj[r][:, 2 * KD_A + h * DV_A:2 * KD_A + (h + 1) * DV_A].astype(BF16)
            qe = (qh * scale * jnp.exp(bh)).astype(BF16)
            ke = (kh * jnp.exp(-bh)).astype(BF16)
            kd = (kh * jnp.exp(bl - bh)).astype(BF16)
            s = jnp.where(inchunk, _dot_nt(qe, ke), 0.0).astype(BF16)
            o_sc[r:r + sub, h * DV_A:(h + 1) * DV_A] = _dot(s, vh)
            qe_sc[r:r + sub, h * DK_A:(h + 1) * DK_A] = qe
            for c in chunks:
                r0, r1 = c * chunk - r, (c + 1) * chunk - r
                u_sc[h, c] = _dot_tn(kd[r0:r1], vh[r0:r1])
                dcol = jnp.broadcast_to(jnp.exp(ends[c]), (LANES, DK_A)).T
                decay[h][c] = jnp.concatenate([dcol] * (DV_A // LANES), axis=1)

    for h in range(H_A):
        st = st_sc[h]
        for c in range(nc):
            stb_sc[h, c] = st.astype(BF16)
            st = st * decay[h][c] + u_sc[h, c]
        st_sc[h] = st

    for c in range(nc):
        r0, r1 = c * chunk, (c + 1) * chunk
        for h in range(H_A):
            o_sc[r0:r1, h * DV_A:(h + 1) * DV_A] += _dot(qe_sc[r0:r1, h * DK_A:(h + 1) * DK_A], stb_sc[h, c])

    for r in blocks:
        gate = proj[r][:, 2 * KD_A + VD_A:2 * KD_A + 2 * VD_A]
        heads = []
        for h in range(H_A):
            heads.append(_rms(o_sc[r:r + sub, h * DV_A:(h + 1) * DV_A], hn_ref[:, h * DV_A:(h + 1) * DV_A]))
        og = (jnp.concatenate(heads, axis=1) * _silu(gate)).astype(BF16)
        y = _dot(og, wout_ref[...])
        y_ref[0, r:r + sub, :] = xs[r] + _rms(y, npost_ref[...])

    @pl.when(t == pl.num_programs(1) - 1)
    def _():
        for h in range(H_A):
            sout_ref[0, h] = st_sc[h]


def _gla_layer(x, s0, npre, win, wglr, wg2, bg, hn, wout, npost, *, rows, chunk):
    bsz, tlen, d = x.shape
    has_state = s0 is not None
    tok = pl.BlockSpec((1, rows, d), lambda b, t: (b, t, 0))
    st_spec = pl.BlockSpec((1, H_A, DK_A, DV_A), lambda b, t: (b, 0, 0, 0))
    consts = [npre, win, wglr, wg2, bg, hn, wout, npost]
    in_specs = [tok] + ([st_spec] if has_state else []) + [_const_spec(a.shape) for a in consts]
    args = [x] + ([s0] if has_state else []) + consts
    return pl.pallas_call(
        functools.partial(_gla_kernel, rows=rows, chunk=chunk, has_state=has_state),
        grid=(bsz, tlen // rows),
        in_specs=in_specs,
        out_specs=(tok, st_spec),
        out_shape=(jax.ShapeDtypeStruct((bsz, tlen, d), F32),
                   jax.ShapeDtypeStruct((bsz, H_A, DK_A, DV_A), F32)),
        scratch_shapes=[pltpu.VMEM((H_A, DK_A, DV_A), F32), pltpu.VMEM((rows, VD_A), F32),
                        pltpu.VMEM((rows, KD_A), BF16),
                        pltpu.VMEM((H_A, rows // chunk, DK_A, DV_A), F32),
                        pltpu.VMEM((H_A, rows // chunk, DK_A, DV_A), BF16)],
        compiler_params=pltpu.CompilerParams(
            dimension_semantics=("parallel", "arbitrary"), vmem_limit_bytes=VMEM_LIMIT),
        name="gla_layer",
    )(*args)


def _kv_kernel(x_ref, cos_ref, sin_ref, nkv_ref, wdc_ref, wdr_ref, nlat_ref, *rest, expand, kblk):
    if expand:
        wuk_ref, wuvt_ref, koh_ref, ckv_ref, kr_ref, kn_ref, kr0_ref, vt_ref = rest
    else:
        ckv_ref, kr_ref = rest
    hb = _rms(x_ref[0], nkv_ref[...]).astype(BF16)
    ckv = _rms(_dot(hb, wdc_ref[...]), nlat_ref[...])
    ckv_ref[0] = ckv
    rr = _dot(hb, wdr_ref[...])
    kr2 = rr[:, :LANES] * cos_ref[...] + rr[:, LANES:] * sin_ref[...]
    kr_ref[0] = kr2[:, :ROPE]
    if expand:
        cb = ckv.astype(BF16)
        kn_ref[0] = _dot(cb, wuk_ref[...]).astype(BF16)
        lane = lax.broadcasted_iota(jnp.int32, kr2.shape, 1)
        kr0_ref[0] = jnp.where(lane < ROPE, kr2, koh_ref[...]).astype(BF16)
        vt = _dot_nt(wuvt_ref[...], cb).astype(BF16)
        for i in range(vt.shape[1] // kblk):
            vt_ref[0, i] = vt[:, i * kblk:(i + 1) * kblk]


def _shared_kv(x, cos2, sin2, nkv, wdc, wdr, nlat, wuk=None, wuvt=None, koh=None, *, rows, kblk=None):
    bsz, tlen, d = x.shape
    expand = wuk is not None
    tok = lambda w: pl.BlockSpec((1, rows, w), lambda b, t: (b, t, 0))
    tab = pl.BlockSpec((rows, LANES), lambda b, t: (t, 0))
    in_specs = [tok(d), tab, tab, _const_spec(nkv.shape), _const_spec(wdc.shape),
                _const_spec(wdr.shape), _const_spec(nlat.shape)]
    args = [x, cos2, sin2, nkv, wdc, wdr, nlat]
    out_specs = [tok(KV_LORA), tok(ROPE)]
    out_shape = [jax.ShapeDtypeStruct((bsz, tlen, KV_LORA), F32),
                 jax.ShapeDtypeStruct((bsz, tlen, ROPE), F32)]
    if expand:
        in_specs += [_const_spec(wuk.shape), _const_spec(wuvt.shape), tab]
        args += [wuk, wuvt, koh]
        out_specs += [tok(H_B * NOPE), tok(LANES),
                      pl.BlockSpec((1, rows // kblk, VD_B, kblk), lambda b, t: (b, t, 0, 0))]
        out_shape += [jax.ShapeDtypeStruct((bsz, tlen, H_B * NOPE), BF16),
                      jax.ShapeDtypeStruct((bsz, tlen, LANES), BF16),
                      jax.ShapeDtypeStruct((bsz, tlen // kblk, VD_B, kblk), BF16)]
    return pl.pallas_call(
        functools.partial(_kv_kernel, expand=expand, kblk=kblk),
        grid=(bsz, tlen // rows),
        in_specs=in_specs,
        out_specs=tuple(out_specs),
        out_shape=tuple(out_shape),
        compiler_params=pltpu.CompilerParams(
            dimension_semantics=("parallel", "parallel"), vmem_limit_bytes=VMEM_LIMIT),
        name="shared_kv",
    )(*args)


def _mla_q(x, cos, sin, npre_ref, win_ref, qn_ref, wq_ref, rope_w):
    hb = _rms(x, npre_ref[...]).astype(BF16)
    proj = _dot(hb, win_ref[...])
    qn = _rms(proj[:, :Q_LORA], qn_ref[...]).astype(BF16)
    qq = _dot(qn, wq_ref[...])
    scale = (NOPE + ROPE) ** -0.5
    n0, n1 = H_B * NOPE, H_B * NOPE + H_B * rope_w
    q_nope = (qq[:, :n0] * scale).astype(BF16)
    q_rope = ((qq[:, n0:n1] * cos + qq[:, n1:] * sin) * scale).astype(BF16)
    return q_nope, q_rope, proj[:, Q_LORA:]


ACC_ROWS = V_B + 16


def _mla_kernel(x_ref, kn_ref, kr_ref, vt_ref, cos_ref, sin_ref, npre_ref, win_ref, qn_ref, wqt_ref,
                wout_ref, npost_ref, y_ref, q_sc, m_sc, acc_sc, alpha_sc, p_sc, *, tq, tk):
    i = pl.program_id(1)
    c = (NOPE + ROPE) ** -0.5 * 1.4426950408889634
    n0, n1 = H_B * NOPE, H_B * (NOPE + ROPE)
    x = x_ref[0]
    hb = _rms(x, npre_ref[...]).astype(BF16)
    qlat = _dot(hb, win_ref[:, :Q_LORA])
    qnt = _rms(qlat, qn_ref[...]).T.astype(BF16)
    qqt = _dot(wqt_ref[...], qnt)
    q_sc[:n0, :] = (qqt[:n0] * c).astype(BF16)
    cos, sin = cos_ref[...], sin_ref[...]
    for h in range(H_B):
        r0 = n0 + h * ROPE
        x1, x2 = qqt[r0:r0 + HALF], qqt[r0 + HALF:r0 + ROPE]
        q_sc[r0:r0 + HALF, :] = ((x1 * cos - x2 * sin) * c).astype(BF16)
        q_sc[r0 + HALF:r0 + ROPE, :] = ((x2 * cos + x1 * sin) * c).astype(BF16)
    row = lax.broadcasted_iota(jnp.int32, (ROPE, tq), 0)
    q_chunk = jnp.right_shift(i * tq + lax.broadcasted_iota(jnp.int32, (ROPE, tq), 1), CHUNK.bit_length() - 1)
    q_sc[n1:, :] = jnp.where(row > q_chunk, MASKED, 0.0).astype(BF16)
    m_sc[...] = jnp.full(m_sc.shape, -jnp.inf, F32)
    acc_sc[...] = jnp.zeros(acc_sc.shape, F32)

    ones = jnp.ones((ACC_ROWS - V_B, tk), BF16)

    def scores(j, h):
        k0 = pl.multiple_of(j * tk, tk)
        kb = jnp.concatenate(
            [kn_ref[0, pl.ds(k0, tk), h * NOPE:(h + 1) * NOPE], kr_ref[0, pl.ds(k0, tk), :]], axis=1)
        qt = jnp.concatenate(
            [q_sc[h * NOPE:(h + 1) * NOPE, :], q_sc[n0 + h * ROPE:n0 + (h + 1) * ROPE, :], q_sc[n1:, :]], axis=0)
        return _dot(kb, qt)

    def softmax(s, h, slot):
        m_prev = m_sc[h]
        m_new = jnp.maximum(m_prev, jnp.max(s, axis=0, keepdims=True))
        alpha_sc[slot, h] = jnp.exp2(m_prev - m_new)
        p_sc[slot, h] = jnp.exp2(s - m_new).astype(BF16)
        m_sc[h] = m_new

    def values(j, h, slot):
        vt = jnp.concatenate([vt_ref[0, j, h * V_B:(h + 1) * V_B, :], ones], axis=0)
        acc_sc[h] = alpha_sc[slot, h] * acc_sc[h] + _dot(vt, p_sc[slot, h])

    for h in range(H_B):
        softmax(scores(0, h), h, 0)

    def block(j, slot_new, slot_old):
        s = [scores(j, h) for h in range(H_B)]
        for h in range(H_B):
            values(j - 1, h, slot_old)
        for h in range(H_B):
            softmax(s[h], h, slot_new)

    def pair(jj, carry):
        block(2 * jj + 1, 1, 0)
        block(2 * jj + 2, 0, 1)
        return carry

    def tail_odd(last):
        block(last, 1, 0)
        for h in range(H_B):
            values(last, h, 1)

    def tail_even(last):
        for h in range(H_B):
            values(last, h, 0)

    if tq == 2 * tk:
        lax.fori_loop(0, i, pair, 0)
        tail_odd(2 * i + 1)
    else:
        lax.fori_loop(0, i // 2, pair, 0)
        odd = lax.rem(i, 2) == 1
        pl.when(odd)(lambda: tail_odd(i))
        pl.when(jnp.logical_not(odd))(lambda: tail_even(i))

    for r in range(0, tq, tk):
        gate = _dot(hb[r:r + tk], win_ref[:, Q_LORA:])
        outs = []
        for h in range(H_B):
            a = acc_sc[h, :, r:r + tk]
            outs.append(a[:V_B] * (1.0 / a[V_B:V_B + 1]))
        o = jnp.concatenate(outs, axis=0).T
        og = (o * _silu(gate)).astype(BF16)
        y = _dot(og, wout_ref[...])
        y_ref[0, r:r + tk, :] = x[r:r + tk] + _rms(y, npost_ref[...])


def _mla_prompt_layer(x, kn, kr0, vt, cos8t, sin8t, npre, win, qn, wqt, wout, npost, *, tq, tk):
    bsz, tlen, d = x.shape
    assert tq in (tk, 2 * tk) and tk % CHUNK == 0 and vt.shape[3] == tk
    tok = pl.BlockSpec((1, tq, d), lambda b, t: (b, t, 0))
    full = lambda w: pl.BlockSpec((1, tlen, w), lambda b, t: (b, 0, 0))
    tab = pl.BlockSpec((cos8t.shape[0], tq), lambda b, t: (0, t))
    return pl.pallas_call(
        functools.partial(_mla_kernel, tq=tq, tk=tk),
        grid=(bsz, tlen // tq),
        in_specs=[tok, full(kn.shape[2]), full(kr0.shape[2]),
                  pl.BlockSpec((1,) + vt.shape[1:], lambda b, t: (b, 0, 0, 0)), tab, tab,
                  _const_spec(npre.shape), _const_spec(win.shape), _const_spec(qn.shape),
                  _const_spec(wqt.shape), _const_spec(wout.shape), _const_spec(npost.shape)],
        out_specs=tok,
        out_shape=jax.ShapeDtypeStruct((bsz, tlen, d), F32),
        scratch_shapes=[pltpu.VMEM((H_B * (NOPE + ROPE) + ROPE, tq), BF16),
                        pltpu.VMEM((H_B, 1, tq), F32),
                        pltpu.VMEM((H_B, ACC_ROWS, tq), F32),
                        pltpu.VMEM((2, H_B, 1, tq), F32),
                        pltpu.VMEM((2, H_B, tk, tq), BF16)],
        compiler_params=pltpu.CompilerParams(
            dimension_semantics=("parallel", "arbitrary"), vmem_limit_bytes=VMEM_LIMIT),
        name="mla_prompt_layer",
    )(x, kn, kr0, vt, cos8t, sin8t, npre, win, qn, wqt, wout, npost)


def _mla_sample_kernel(x_ref, cckv_ref, ckr_ref, nckv_ref, nkr_ref, cos_ref, sin_ref, npre_ref, win_ref,
                       qn_ref, wq_ref, wuk_ref, wuv_ref, wout_ref, npost_ref, y_ref, *, past):
    x = x_ref[0]
    t = x.shape[0]
    q_nope, q_rope, gate = _mla_q(x, cos_ref[...], sin_ref[...], npre_ref, win_ref, qn_ref, wq_ref, LANES)

    qlat = jnp.concatenate(
        [_dot_nt(q_nope[:, h * NOPE:(h + 1) * NOPE], wuk_ref[:, h * NOPE:(h + 1) * NOPE]) for h in range(H_B)],
        axis=0).astype(BF16)
    qr = jnp.concatenate([q_rope[:, h * LANES:h * LANES + ROPE] for h in range(H_B)], axis=0)

    c_ckv = cckv_ref[0].astype(BF16)
    n_ckv = nckv_ref[0].astype(BF16)
    s_c = _dot_nt(qlat, c_ckv) + _dot_nt(qr, ckr_ref[0].astype(BF16))
    s_n = _dot_nt(qlat, n_ckv) + _dot_nt(qr, nkr_ref[0].astype(BF16))

    cshift = CHUNK.bit_length() - 1

    def chunk_mask(s, k_start):
        rows = lax.broadcasted_iota(jnp.int32, s.shape, 0)
        cols = lax.broadcasted_iota(jnp.int32, s.shape, 1)
        q_chunk = jnp.right_shift(past + jnp.bitwise_and(rows, t - 1), cshift)
        k_chunk = jnp.right_shift(k_start + cols, cshift)
        return jnp.where(k_chunk <= q_chunk, s, -jnp.inf)

    s_c = chunk_mask(s_c, 0)
    s_n = chunk_mask(s_n, past)
    m = jnp.maximum(jnp.max(s_c, axis=-1, keepdims=True), jnp.max(s_n, axis=-1, keepdims=True))
    p_c = jnp.exp(s_c - m)
    p_n = jnp.exp(s_n - m)
    l = jnp.sum(p_c, axis=-1, keepdims=True) + jnp.sum(p_n, axis=-1, keepdims=True)
    olat = (_dot(p_c.astype(BF16), c_ckv) + _dot(p_n.astype(BF16), n_ckv)) * (1.0 / l)
    olat = olat.astype(BF16)
    o = jnp.concatenate(
        [_dot(olat[h * t:(h + 1) * t], wuv_ref[:, h * V_B:(h + 1) * V_B]) for h in range(H_B)], axis=1)
    og = (o * _silu(gate)).astype(BF16)
    y = _dot(og, wout_ref[...])
    y_ref[0] = x + _rms(y, npost_ref[...])


def _mla_sample_layer(x, cache_ckv, cache_kr, new_ckv, new_kr, cos, sin, npre, win, qn, wq, wuk, wuv,
                      wout, npost):
    bsz, t, d = x.shape
    past = cache_ckv.shape[1]
    assert t & (t - 1) == 0
    per_b = lambda a: pl.BlockSpec((1,) + a.shape[1:], lambda b: (b, 0, 0))
    consts = [cos, sin, npre, win, qn, wq, wuk, wuv, wout, npost]
    return pl.pallas_call(
        functools.partial(_mla_sample_kernel, past=past),
        grid=(bsz,),
        in_specs=[per_b(x), per_b(cache_ckv), per_b(cache_kr), per_b(new_ckv), per_b(new_kr)]
                 + [_const_spec(a.shape) for a in consts],
        out_specs=per_b(x),
        out_shape=jax.ShapeDtypeStruct((bsz, t, d), F32),
        compiler_params=pltpu.CompilerParams(
            dimension_semantics=("parallel",), vmem_limit_bytes=VMEM_LIMIT),
        name="mla_sample_layer",
    )(x, cache_ckv, cache_kr, new_ckv, new_kr, *consts)


def _rope_tables(pos_offset, tlen):
    inv = jnp.power(ROPE_BASE, -jnp.arange(HALF, dtype=F32) / HALF)
    ang = (pos_offset + jnp.arange(tlen)).astype(F32)[:, None] * inv[None, :]
    cos = jnp.cos(ang)
    sin = jnp.sin(ang)
    return jnp.concatenate([cos, cos], axis=1), jnp.concatenate([sin, sin], axis=1)


def _rot_cols(w):
    return jnp.concatenate([-w[..., HALF:], w[..., :HALF]], axis=-1)


def _row(a):
    return a.reshape(1, -1)


def kernel(x_prompt, x_sample, state_gla, cache_ckv, cache_krope, a_norm_pre, a_w_in, a_w_gate2, a_b_gate,
           a_head_norm, a_w_out, a_norm_post, kv_norm, kv_w_down, kv_latent_norm, kv_w_uk, kv_w_uv,
           b_norm_pre, b_w_in, b_q_norm, b_w_uq, b_w_out, b_norm_post):
    past = cache_ckv.shape[1]
    t_p, t_s = x_prompt.shape[1], x_sample.shape[1]

    n_main = 2 * KD_A + 2 * VD_A
    a_win = a_w_in[:, :, :n_main].astype(BF16)
    a_wglr = jnp.pad(a_w_in[:, :, n_main:], ((0, 0), (0, 0), (0, 2 * LANES - GATE_RANK))).astype(BF16)
    a_wg2 = a_w_gate2.astype(BF16)
    a_wout = a_w_out.astype(BF16)
    wdc = kv_w_down[:, :KV_LORA].astype(BF16)
    wdr_cols = kv_w_down[:, KV_LORA:]
    wdr = jnp.concatenate([wdr_cols, wdr_cols, _rot_cols(wdr_cols), _rot_cols(wdr_cols)], axis=1).astype(BF16)
    wuk = kv_w_uk.astype(BF16)
    wuv = kv_w_uv.astype(BF16)
    b_win = b_w_in.astype(BF16)
    b_wout = b_w_out.astype(BF16)
    uq = b_w_uq.reshape(N_B, Q_LORA, H_B, NOPE + ROPE)
    uq_nope = uq[..., :NOPE].reshape(N_B, Q_LORA, H_B * NOPE)
    uq_rope = uq[..., NOPE:]
    uq_rot = _rot_cols(uq_rope)
    wqt_p = jnp.concatenate([uq_nope, uq_rope.reshape(N_B, Q_LORA, H_B * ROPE)],
                            axis=-1).astype(BF16).transpose(0, 2, 1)
    wuvt = wuv.T
    pad = lambda w: jnp.pad(w, ((0, 0), (0, 0), (0, 0), (0, LANES - ROPE))).reshape(N_B, Q_LORA, H_B * LANES)
    wq_s = jnp.concatenate([uq_nope, pad(uq_rope), pad(uq_rot)], axis=-1).astype(BF16)

    cos_p, sin_p = _rope_tables(0, t_p)
    cos_s, sin_s = _rope_tables(past, t_s)
    tile2 = lambda a: jnp.concatenate([a, a], axis=1)
    cos8t_p, sin8t_p = cos_p[:, :HALF].T, sin_p[:, :HALF].T
    padl = lambda a: jnp.tile(jnp.pad(a, ((0, 0), (0, LANES - ROPE))), (1, H_B))
    cos8_s, sin8_s = padl(cos_s), padl(sin_s)

    def gla_stack(x, states, rows, chunk):
        outs = []
        for l in range(N_A):
            s0 = None if states is None else states[l]
            x, s_new = _gla_layer(x, s0, _row(a_norm_pre[l]), a_win[l], a_wglr[l], a_wg2[l], _row(a_b_gate[l]),
                                  _row(a_head_norm[l]), a_wout[l], _row(a_norm_post[l]),
                                  rows=rows, chunk=chunk)
            outs.append(s_new)
        return x, jnp.stack(outs)

    x, st_p = gla_stack(x_prompt, None, GLA_ROWS, CHUNK)
    assert t_p // CHUNK <= LANES - ROPE
    koh = jax.nn.one_hot(ROPE + jnp.arange(t_p) // CHUNK, LANES, dtype=F32)
    ckv_p, kr_p, kn, kr0, vt = _shared_kv(x, tile2(cos_p), tile2(sin_p), _row(kv_norm), wdc, wdr,
                                          _row(kv_latent_norm), wuk, wuvt, koh, rows=KV_ROWS, kblk=ATT_K)
    for j in range(N_B):
        x = _mla_prompt_layer(x, kn, kr0, vt, cos8t_p, sin8t_p, _row(b_norm_pre[j]), b_win[j],
                              _row(b_q_norm[j]), wqt_p[j], b_wout[j], _row(b_norm_post[j]),
                              tq=ATT_Q, tk=ATT_K)
    y_prompt = x

    x, st_s = gla_stack(x_sample, state_gla, t_s, min(CHUNK, t_s))
    ckv_s, kr_s = _shared_kv(x, tile2(cos_s), tile2(sin_s), _row(kv_norm), wdc, wdr,
                             _row(kv_latent_norm), rows=t_s)
    for j in range(N_B):
        x = _mla_sample_layer(x, cache_ckv, cache_krope, ckv_s, kr_s, cos8_s, sin8_s,
                              _row(b_norm_pre[j]), b_win[j], _row(b_q_norm[j]), wq_s[j], wuk, wuv,
                              b_wout[j], _row(b_norm_post[j]))
    y_sample = x

    return (y_prompt, y_sample, st_p, st_s, ckv_p, kr_p, ckv_s, kr_s)
```

```python
import functools

import jax
import jax.numpy as jnp
from jax import lax
from jax.experimental import pallas as pl
from jax.experimental.pallas import tpu as pltpu

F32 = jnp.float32
BF16 = jnp.bfloat16

D_MODEL = 1024
CHUNK = 64
N_A = 2
N_B = 2
H_A = 4
DK_A = 128
DV_A = 256
KD_A = H_A * DK_A
VD_A = H_A * DV_A
GATE_RANK = 16
GATE_TAU = 16.0
H_B = 8
NOPE = 128
ROPE = 64
HALF = ROPE // 2
V_B = 128
VD_B = H_B * V_B
Q_LORA = 384
KV_LORA = 256
ROPE_BASE = 10000.0
EPS = 1e-6

LANES = 128
VMEM_LIMIT = 56 * 1024 * 1024

GLA_ROWS = 512
GLA_SUB = 256
SAMPLE_GROUPS = 2
ATT_Q = 512
ATT_K = 256
MASKED = -1e30


def _dot(a, b):
    return jnp.dot(a, b, preferred_element_type=F32)


def _dot_nt(a, b):
    return lax.dot_general(a, b, (((1,), (1,)), ((), ())), preferred_element_type=F32)


def _dot_tn(a, b):
    return lax.dot_general(a, b, (((0,), (0,)), ((), ())), preferred_element_type=F32)


def _rms(xf, g):
    ms = jnp.mean(xf * xf, axis=-1, keepdims=True)
    return xf * lax.rsqrt(ms + EPS) * g


def _silu(g):
    return g * (1.0 / (1.0 + jnp.exp(-g)))


def _const_spec(shape):
    return pl.BlockSpec(shape, lambda *_: (0,) * len(shape))


def _kv_latent(xb, cos, sin, nkv_ref, wdc_ref, wdr_ref, nlat_ref):
    hb = _rms(xb, nkv_ref[...]).astype(BF16)
    ckv = _rms(_dot(hb, wdc_ref[...]), nlat_ref[...])
    rr = _dot(hb, wdr_ref[...])
    return ckv, rr[:, :LANES] * cos + rr[:, LANES:] * sin


def _gla_kernel(*refs, rows, chunk, carry, n_prev, kv_blk):
    refs = list(refs)
    x_ref = refs.pop(0)
    s0_ref = None if carry else refs.pop(0)
    prev_ref = refs.pop(0) if n_prev else None
    npre_ref, win_ref, wglr_ref, wg2_ref, bg_ref, hn_ref, wout_ref, npost_ref = refs[:8]
    del refs[:8]
    if kv_blk:
        (cos_ref, sin_ref, koh_ref, nkv_ref, wdc_ref, wdr_ref, nlat_ref, wuk_ref, wuvt_ref) = refs[:9]
        del refs[:9]
    y_ref, sout_ref = refs[:2]
    del refs[:2]
    if kv_blk:
        ckv_ref, kr_ref, kn_ref, kr0_ref, vt_ref = refs[:5]
        del refs[:5]
    st_sc = refs.pop(0) if carry else None
    o_sc, qe_sc, u_sc, stb_sc = refs
    t = pl.program_id(1)

    if carry:
        @pl.when(t == 0)
        def _():
            st_sc[...] = jnp.zeros(st_sc.shape, F32)

    shift = chunk.bit_length() - 1
    sub = min(rows, GLA_SUB)
    blocks = range(0, rows, sub)
    xs, z, proj, b = {}, {}, {}, {}
    for r in blocks:
        xs[r] = x_ref[0, r:r + sub, :]
        hb = _rms(xs[r], npre_ref[...]).astype(BF16)
        glr = _dot(hb, wglr_ref[...])[:, :GATE_RANK].astype(BF16)
        z[r] = _dot(glr, wg2_ref[...]) + bg_ref[...]
        proj[r] = _dot(hb, win_ref[...])

    ri = lax.broadcasted_iota(jnp.int32, (sub, sub), 0)
    ci = lax.broadcasted_iota(jnp.int32, (sub, sub), 1)
    inchunk = jnp.where(jnp.right_shift(ri, shift) == jnp.right_shift(ci, shift), ci, sub) <= ri
    tri = jnp.where(inchunk, 1.0, 0.0).astype(BF16)
    for r in blocks:
        g = (jnp.minimum(z[r], 0.0) - jnp.log(1.0 + jnp.exp(-jnp.abs(z[r])))) * (1.0 / GATE_TAU)
        g_hi = g.astype(BF16)
        g_lo = (g - g_hi.astype(F32)).astype(BF16)
        b[r] = _dot(tri, g_hi) + _dot(tri, g_lo)

    scale = DK_A ** -0.5
    nc = rows // chunk
    decay = [[None] * nc for _ in range(H_A)]

    for h in range(H_A):
        for r in blocks:
            chunks = range(r // chunk, (r + sub) // chunk)
            bh = b[r][:, h * DK_A:(h + 1) * DK_A]
            ends = {c: bh[(c + 1) * chunk - 1 - r:(c + 1) * chunk - r, :] for c in chunks}
            bl = jnp.concatenate([jnp.broadcast_to(ends[c], (chunk, DK_A)) for c in chunks], axis=0)
            qh = proj[r][:, h * DK_A:(h + 1) * DK_A]
            kh = proj[r][:, KD_A + h * DK_A:KD_A + (h + 1) * DK_A]
            vh = proj[r][:, 2 * KD_A + h * DV_A:2 * KD_A + (h + 1) * DV_A].astype(BF16)
            qe = (qh * scale * jnp.exp(bh)).astype(BF16)
            ke = (kh * jnp.exp(-bh)).astype(BF16)
            kd = (kh * jnp.exp(bl - bh)).astype(BF16)
            s = jnp.where(inchunk, _dot_nt(qe, ke), 0.0).astype(BF16)
            o_sc[r:r + sub, h * DV_A:(h + 1) * DV_A] = _dot(s, vh)
            qe_sc[r:r + sub, h * DK_A:(h + 1) * DK_A] = qe
            for c in chunks:
                r0, r1 = c * chunk - r, (c + 1) * chunk - r
                u_sc[h, c] = _dot_tn(kd[r0:r1], vh[r0:r1])
                dcol = jnp.broadcast_to(jnp.exp(ends[c]), (LANES, DK_A)).T
                decay[h][c] = jnp.concatenate([dcol] * (DV_A // LANES), axis=1)

    for h in range(H_A):
        if carry:
            st = st_sc[h]
        for c in range(nc):
            if not carry:
                st = s0_ref[c, h]
            stb_sc[h, c] = st.astype(BF16)
            st_new = st * decay[h][c] + u_sc[h, c]
            if carry:
                st = st_new
            else:
                sout_ref[c, h] = st_new
        if carry:
            st_sc[h] = st

    for c in range(nc):
        r0, r1 = c * chunk, (c + 1) * chunk
        for h in range(H_A):
            o_sc[r0:r1, h * DV_A:(h + 1) * DV_A] += _dot(qe_sc[r0:r1, h * DK_A:(h + 1) * DK_A], stb_sc[h, c])

    for r in blocks:
        gate = proj[r][:, 2 * KD_A + VD_A:2 * KD_A + 2 * VD_A]
        heads = []
        for h in range(H_A):
            heads.append(_rms(o_sc[r:r + sub, h * DV_A:(h + 1) * DV_A], hn_ref[:, h * DV_A:(h + 1) * DV_A]))
        og = (jnp.concatenate(heads, axis=1) * _silu(gate)).astype(BF16)
        y = _dot(og, wout_ref[...])
        x_new = xs[r] + _rms(y, npost_ref[...])
        y_ref[0, r:r + sub, :] = x_new
        if kv_blk:
            ckv, kr2 = _kv_latent(x_new, cos_ref[r:r + sub, :], sin_ref[r:r + sub, :],
                                  nkv_ref, wdc_ref, wdr_ref, nlat_ref)
            ckv_ref[0, r:r + sub, :] = ckv
            kr_ref[0, r:r + sub, :] = kr2[:, :ROPE]
            cb = ckv.astype(BF16)
            kn_ref[0, r:r + sub, :] = _dot(cb, wuk_ref[...]).astype(BF16)
            lane = lax.broadcasted_iota(jnp.int32, kr2.shape, 1)
            kr0_ref[0, r:r + sub, :] = jnp.where(lane < ROPE, kr2, koh_ref[r:r + sub, :]).astype(BF16)
            vt = _dot_nt(wuvt_ref[...], cb).astype(BF16)
            for i in range(sub // kv_blk):
                vt_ref[0, r // kv_blk + i] = vt[:, i * kv_blk:(i + 1) * kv_blk]

    if carry:
        @pl.when(t == pl.num_programs(1) - 1)
        def _():
            for li in range(n_prev):
                sout_ref[li, 0] = prev_ref[li, 0]
            for h in range(H_A):
                sout_ref[n_prev, 0, h] = st_sc[h]


def _gla_scratch(rows, chunk):
    return [pltpu.VMEM((rows, VD_A), F32), pltpu.VMEM((rows, KD_A), BF16),
            pltpu.VMEM((H_A, rows // chunk, DK_A, DV_A), F32),
            pltpu.VMEM((H_A, rows // chunk, DK_A, DV_A), BF16)]


def _gla_prompt_layer(x, prev, weights, kv=None, *, rows, chunk, kv_blk=None):
    bsz, tlen, d = x.shape
    n_prev = 0 if prev is None else prev.shape[0]
    tok = lambda w: pl.BlockSpec((1, rows, w), lambda b, t: (b, t, 0))
    st_spec = lambda n: pl.BlockSpec((n, 1, H_A, DK_A, DV_A), lambda b, t: (0, b, 0, 0, 0))
    in_specs = [tok(d)] + ([st_spec(n_prev)] if n_prev else []) + [_const_spec(a.shape) for a in weights]
    args = [x] + ([prev] if n_prev else []) + list(weights)
    out_specs = [tok(d), st_spec(n_prev + 1)]
    out_shape = [jax.ShapeDtypeStruct((bsz, tlen, d), F32),
                 jax.ShapeDtypeStruct((n_prev + 1, bsz, H_A, DK_A, DV_A), F32)]
    if kv is not None:
        assert min(rows, GLA_SUB) % kv_blk == 0
        tables, kv_weights = kv[:3], kv[3:]
        tab = pl.BlockSpec((rows, LANES), lambda b, t: (t, 0))
        in_specs += [tab] * 3 + [_const_spec(a.shape) for a in kv_weights]
        args += list(tables) + list(kv_weights)
        out_specs += [tok(KV_LORA), tok(ROPE), tok(H_B * NOPE), tok(LANES),
                      pl.BlockSpec((1, rows // kv_blk, VD_B, kv_blk), lambda b, t: (b, t, 0, 0))]
        out_shape += [jax.ShapeDtypeStruct((bsz, tlen, KV_LORA), F32),
                      jax.ShapeDtypeStruct((bsz, tlen, ROPE), F32),
                      jax.ShapeDtypeStruct((bsz, tlen, H_B * NOPE), BF16),
                      jax.ShapeDtypeStruct((bsz, tlen, LANES), BF16),
                      jax.ShapeDtypeStruct((bsz, tlen // kv_blk, VD_B, kv_blk), BF16)]
    return pl.pallas_call(
        functools.partial(_gla_kernel, rows=rows, chunk=chunk, carry=True, n_prev=n_prev,
                          kv_blk=kv_blk if kv is not None else None),
        grid=(bsz, tlen // rows),
        in_specs=in_specs,
        out_specs=tuple(out_specs),
        out_shape=tuple(out_shape),
        scratch_shapes=[pltpu.VMEM((H_A, DK_A, DV_A), F32)] + _gla_scratch(rows, chunk),
        compiler_params=pltpu.CompilerParams(
            dimension_semantics=("parallel", "arbitrary"), vmem_limit_bytes=VMEM_LIMIT),
        name="gla_layer",
    )(*args)


def _gla_sample_layer(x, s0, weights, *, groups):
    bsz, t, d = x.shape
    per = bsz // groups
    rows = per * t
    tok = pl.BlockSpec((1, rows, d), lambda g, _: (g, 0, 0))
    st_spec = pl.BlockSpec((per, H_A, DK_A, DV_A), lambda g, _: (g, 0, 0, 0))
    y, s_new = pl.pallas_call(
        functools.partial(_gla_kernel, rows=rows, chunk=t, carry=False, n_prev=0, kv_blk=None),
        grid=(groups, 1),
        in_specs=[tok, st_spec] + [_const_spec(a.shape) for a in weights],
        out_specs=(tok, st_spec),
        out_shape=(jax.ShapeDtypeStruct((groups, rows, d), F32),
                   jax.ShapeDtypeStruct((bsz, H_A, DK_A, DV_A), F32)),
        scratch_shapes=_gla_scratch(rows, t),
        compiler_params=pltpu.CompilerParams(
            dimension_semantics=("parallel", "arbitrary"), vmem_limit_bytes=VMEM_LIMIT),
        name="gla_sample_layer",
    )(x.reshape(groups, rows, d), s0, *weights)
    return y.reshape(bsz, t, d), s_new


def _kv_kernel(x_ref, cos_ref, sin_ref, nkv_ref, wdc_ref, wdr_ref, nlat_ref, ckv_ref, kr_ref):
    ckv, kr2 = _kv_latent(x_ref[...], cos_ref[...], sin_ref[...], nkv_ref, wdc_ref, wdr_ref, nlat_ref)
    ckv_ref[...] = ckv
    kr_ref[...] = kr2[:, :ROPE]


def _shared_kv_latent(x, cos2, sin2, nkv, wdc, wdr, nlat):
    n, d = x.shape
    args = [x, cos2, sin2, nkv, wdc, wdr, nlat]
    return pl.pallas_call(
        _kv_kernel,
        grid=(1,),
        in_specs=[_const_spec(a.shape) for a in args],
        out_specs=(_const_spec((n, KV_LORA)), _const_spec((n, ROPE))),
        out_shape=(jax.ShapeDtypeStruct((n, KV_LORA), F32), jax.ShapeDtypeStruct((n, ROPE), F32)),
        compiler_params=pltpu.CompilerParams(
            dimension_semantics=("arbitrary",), vmem_limit_bytes=VMEM_LIMIT),
        name="shared_kv",
    )(*args)


def _mla_q(x, cos, sin, npre_ref, win_ref, qn_ref, wq_ref, rope_w):
    hb = _rms(x, npre_ref[...]).astype(BF16)
    proj = _dot(hb, win_ref[...])
    qn = _rms(proj[:, :Q_LORA], qn_ref[...]).astype(BF16)
    qq = _dot(qn, wq_ref[...])
    scale = (NOPE + ROPE) ** -0.5
    n0, n1 = H_B * NOPE, H_B * NOPE + H_B * rope_w
    q_nope = (qq[:, :n0] * scale).astype(BF16)
    q_rope = ((qq[:, n0:n1] * cos + qq[:, n1:] * sin) * scale).astype(BF16)
    return q_nope, q_rope, proj[:, Q_LORA:]


ACC_ROWS = V_B + 16


def _mla_kernel(x_ref, kn_ref, kr_ref, vt_ref, cos_ref, sin_ref, npre_ref, win_ref, qn_ref, wqt_ref,
                wout_ref, npost_ref, y_ref, q_sc, m_sc, acc_sc, alpha_sc, p_sc, *, tq, tk):
    i = pl.program_id(1)
    c = (NOPE + ROPE) ** -0.5 * 1.4426950408889634
    n0, n1 = H_B * NOPE, H_B * (NOPE + ROPE)
    x = x_ref[0]
    hb = _rms(x, npre_ref[...]).astype(BF16)
    qlat = _dot(hb, win_ref[:, :Q_LORA])
    qnt = _rms(qlat, qn_ref[...]).T.astype(BF16)
    qqt = _dot(wqt_ref[...], qnt)
    q_sc[:n0, :] = (qqt[:n0] * c).astype(BF16)
    cos, sin = cos_ref[...], sin_ref[...]
    for h in range(H_B):
        r0 = n0 + h * ROPE
        x1, x2 = qqt[r0:r0 + HALF], qqt[r0 + HALF:r0 + ROPE]
        q_sc[r0:r0 + HALF, :] = ((x1 * cos - x2 * sin) * c).astype(BF16)
        q_sc[r0 + HALF:r0 + ROPE, :] = ((x2 * cos + x1 * sin) * c).astype(BF16)
    row = lax.broadcasted_iota(jnp.int32, (ROPE, tq), 0)
    q_chunk = jnp.right_shift(i * tq + lax.broadcasted_iota(jnp.int32, (ROPE, tq), 1), CHUNK.bit_length() - 1)
    q_sc[n1:, :] = jnp.where(row > q_chunk, MASKED, 0.0).astype(BF16)
    m_sc[...] = jnp.full(m_sc.shape, -jnp.inf, F32)
    acc_sc[...] = jnp.zeros(acc_sc.shape, F32)

    ones = jnp.ones((ACC_ROWS - V_B, tk), BF16)

    def scores(j, h):
        k0 = pl.multiple_of(j * tk, tk)
        kb = jnp.concatenate(
            [kn_ref[0, pl.ds(k0, tk), h * NOPE:(h + 1) * NOPE], kr_ref[0, pl.ds(k0, tk), :]], axis=1)
        qt = jnp.concatenate(
            [q_sc[h * NOPE:(h + 1) * NOPE, :], q_sc[n0 + h * ROPE:n0 + (h + 1) * ROPE, :], q_sc[n1:, :]], axis=0)
        return _dot(kb, qt)

    def softmax(s, h, slot):
        m_prev = m_sc[h]
        m_new = jnp.maximum(m_prev, jnp.max(s, axis=0, keepdims=True))
        alpha_sc[slot, h] = jnp.exp2(m_prev - m_new)
        p_sc[slot, h] = jnp.exp2(s - m_new).astype(BF16)
        m_sc[h] = m_new

    def values(j, h, slot):
        vt = jnp.concatenate([vt_ref[0, j, h * V_B:(h + 1) * V_B, :], ones], axis=0)
        acc_sc[h] = alpha_sc[slot, h] * acc_sc[h] + _dot(vt, p_sc[slot, h])

    for h in range(H_B):
        softmax(scores(0, h), h, 0)

    def block(j, slot_new, slot_old):
        s = [scores(j, h) for h in range(H_B)]
        for h in range(H_B):
            values(j - 1, h, slot_old)
        for h in range(H_B):
            softmax(s[h], h, slot_new)

    def pair(jj, carry):
        block(2 * jj + 1, 1, 0)
        block(2 * jj + 2, 0, 1)
        return carry

    def tail_odd(last):
        block(last, 1, 0)
        for h in range(H_B):
            values(last, h, 1)

    def tail_even(last):
        for h in range(H_B):
            values(last, h, 0)

    if tq == 2 * tk:
        lax.fori_loop(0, i, pair, 0)
        tail_odd(2 * i + 1)
    else:
        lax.fori_loop(0, i // 2, pair, 0)
        odd = lax.rem(i, 2) == 1
        pl.when(odd)(lambda: tail_odd(i))
        pl.when(jnp.logical_not(odd))(lambda: tail_even(i))

    for r in range(0, tq, tk):
        gate = _dot(hb[r:r + tk], win_ref[:, Q_LORA:])
        outs = []
        for h in range(H_B):
            a = acc_sc[h, :, r:r + tk]
            outs.append(a[:V_B] * (1.0 / a[V_B:V_B + 1]))
        o = jnp.concatenate(outs, axis=0).T
        og = (o * _silu(gate)).astype(BF16)
        y = _dot(og, wout_ref[...])
        y_ref[0, r:r + tk, :] = x[r:r + tk] + _rms(y, npost_ref[...])


def _mla_prompt_layer(x, kn, kr0, vt, cos8t, sin8t, npre, win, qn, wqt, wout, npost, *, tq, tk):
    bsz, tlen, d = x.shape
    assert tq in (tk, 2 * tk) and tk % CHUNK == 0 and vt.shape[3] == tk
    tok = pl.BlockSpec((1, tq, d), lambda b, t: (b, t, 0))
    full = lambda w: pl.BlockSpec((1, tlen, w), lambda b, t: (b, 0, 0))
    tab = pl.BlockSpec((cos8t.shape[0], tq), lambda b, t: (0, t))
    return pl.pallas_call(
        functools.partial(_mla_kernel, tq=tq, tk=tk),
        grid=(bsz, tlen // tq),
        in_specs=[tok, full(kn.shape[2]), full(kr0.shape[2]),
                  pl.BlockSpec((1,) + vt.shape[1:], lambda b, t: (b, 0, 0, 0)), tab, tab,
                  _const_spec(npre.shape), _const_spec(win.shape), _const_spec(qn.shape),
                  _const_spec(wqt.shape), _const_spec(wout.shape), _const_spec(npost.shape)],
        out_specs=tok,
        out_shape=jax.ShapeDtypeStruct((bsz, tlen, d), F32),
        scratch_shapes=[pltpu.VMEM((H_B * (NOPE + ROPE) + ROPE, tq), BF16),
                        pltpu.VMEM((H_B, 1, tq), F32),
                        pltpu.VMEM((H_B, ACC_ROWS, tq), F32),
                        pltpu.VMEM((2, H_B, 1, tq), F32),
                        pltpu.VMEM((2, H_B, tk, tq), BF16)],
        compiler_params=pltpu.CompilerParams(
            dimension_semantics=("parallel", "arbitrary"), vmem_limit_bytes=VMEM_LIMIT),
        name="mla_prompt_layer",
    )(x, kn, kr0, vt, cos8t, sin8t, npre, win, qn, wqt, wout, npost)


def _mla_sample_kernel(x_ref, cckv_ref, ckr_ref, nckv_ref, nkr_ref, cos_ref, sin_ref, npre_ref, win_ref,
                       qn_ref, wq_ref, wuk_ref, wuv_ref, wout_ref, npost_ref, y_ref, *, past):
    x = x_ref[0]
    t = x.shape[0]
    q_nope, q_rope, gate = _mla_q(x, cos_ref[...], sin_ref[...], npre_ref, win_ref, qn_ref, wq_ref, LANES)

    qlat = jnp.concatenate(
        [_dot_nt(q_nope[:, h * NOPE:(h + 1) * NOPE], wuk_ref[:, h * NOPE:(h + 1) * NOPE]) for h in range(H_B)],
        axis=0).astype(BF16)
    qr = jnp.concatenate([q_rope[:, h * LANES:h * LANES + ROPE] for h in range(H_B)], axis=0)

    c_ckv = cckv_ref[0].astype(BF16)
    n_ckv = nckv_ref[0].astype(BF16)
    s_c = _dot_nt(qlat, c_ckv) + _dot_nt(qr, ckr_ref[0].astype(BF16))
    s_n = _dot_nt(qlat, n_ckv) + _dot_nt(qr, nkr_ref[0].astype(BF16))

    cshift = CHUNK.bit_length() - 1

    def chunk_mask(s, k_start):
        rows = lax.broadcasted_iota(jnp.int32, s.shape, 0)
        cols = lax.broadcasted_iota(jnp.int32, s.shape, 1)
        q_chunk = jnp.right_shift(past + jnp.bitwise_and(rows, t - 1), cshift)
        k_chunk = jnp.right_shift(k_start + cols, cshift)
        return jnp.where(k_chunk <= q_chunk, s, -jnp.inf)

    s_c = chunk_mask(s_c, 0)
    s_n = chunk_mask(s_n, past)
    m = jnp.maximum(jnp.max(s_c, axis=-1, keepdims=True), jnp.max(s_n, axis=-1, keepdims=True))
    p_c = jnp.exp(s_c - m)
    p_n = jnp.exp(s_n - m)
    l = jnp.sum(p_c, axis=-1, keepdims=True) + jnp.sum(p_n, axis=-1, keepdims=True)
    olat = (_dot(p_c.astype(BF16), c_ckv) + _dot(p_n.astype(BF16), n_ckv)) * (1.0 / l)
    olat = olat.astype(BF16)
    o = jnp.concatenate(
        [_dot(olat[h * t:(h + 1) * t], wuv_ref[:, h * V_B:(h + 1) * V_B]) for h in range(H_B)], axis=1)
    og = (o * _silu(gate)).astype(BF16)
    y = _dot(og, wout_ref[...])
    y_ref[0] = x + _rms(y, npost_ref[...])


def _mla_sample_layer(x, cache_ckv, cache_kr, new_ckv, new_kr, cos, sin, npre, win, qn, wq, wuk, wuv,
                      wout, npost):
    bsz, t, d = x.shape
    past = cache_ckv.shape[1]
    assert t & (t - 1) == 0
    per_b = lambda a: pl.BlockSpec((1,) + a.shape[1:], lambda b: (b, 0, 0))
    consts = [cos, sin, npre, win, qn, wq, wuk, wuv, wout, npost]
    return pl.pallas_call(
        functools.partial(_mla_sample_kernel, past=past),
        grid=(bsz,),
        in_specs=[per_b(x), per_b(cache_ckv), per_b(cache_kr), per_b(new_ckv), per_b(new_kr)]
                 + [_const_spec(a.shape) for a in consts],
        out_specs=per_b(x),
        out_shape=jax.ShapeDtypeStruct((bsz, t, d), F32),
        compiler_params=pltpu.CompilerParams(
            dimension_semantics=("parallel",), vmem_limit_bytes=VMEM_LIMIT),
        name="mla_sample_layer",
    )(x, cache_ckv, cache_kr, new_ckv, new_kr, *consts)


def _rope_tables(pos_offset, tlen):
    inv = jnp.power(ROPE_BASE, -jnp.arange(HALF, dtype=F32) / HALF)
    ang = (pos_offset + jnp.arange(tlen)).astype(F32)[:, None] * inv[None, :]
    cos = jnp.cos(ang)
    sin = jnp.sin(ang)
    return jnp.concatenate([cos, cos], axis=1), jnp.concatenate([sin, sin], axis=1)


def _rot_cols(w):
    return jnp.concatenate([-w[..., HALF:], w[..., :HALF]], axis=-1)


def _row(a):
    return a.reshape(1, -1)


def kernel(x_prompt, x_sample, state_gla, cache_ckv, cache_krope, a_norm_pre, a_w_in, a_w_gate2, a_b_gate,
           a_head_norm, a_w_out, a_norm_post, kv_norm, kv_w_down, kv_latent_norm, kv_w_uk, kv_w_uv,
           b_norm_pre, b_w_in, b_q_norm, b_w_uq, b_w_out, b_norm_post):
    past = cache_ckv.shape[1]
    t_p, t_s = x_prompt.shape[1], x_sample.shape[1]

    n_main = 2 * KD_A + 2 * VD_A
    a_win = a_w_in[:, :, :n_main].astype(BF16)
    a_wglr = jnp.pad(a_w_in[:, :, n_main:], ((0, 0), (0, 0), (0, 2 * LANES - GATE_RANK))).astype(BF16)
    a_wg2 = a_w_gate2.astype(BF16)
    a_wout = a_w_out.astype(BF16)
    wdc = kv_w_down[:, :KV_LORA].astype(BF16)
    wdr_cols = kv_w_down[:, KV_LORA:]
    wdr = jnp.concatenate([wdr_cols, wdr_cols, _rot_cols(wdr_cols), _rot_cols(wdr_cols)], axis=1).astype(BF16)
    wuk = kv_w_uk.astype(BF16)
    wuv = kv_w_uv.astype(BF16)
    b_win = b_w_in.astype(BF16)
    b_wout = b_w_out.astype(BF16)
    uq = b_w_uq.reshape(N_B, Q_LORA, H_B, NOPE + ROPE)
    uq_nope = uq[..., :NOPE].reshape(N_B, Q_LORA, H_B * NOPE)
    uq_rope = uq[..., NOPE:]
    uq_rot = _rot_cols(uq_rope)
    wqt_p = jnp.concatenate([uq_nope, uq_rope.reshape(N_B, Q_LORA, H_B * ROPE)],
                            axis=-1).astype(BF16).transpose(0, 2, 1)
    wuvt = wuv.T
    pad = lambda w: jnp.pad(w, ((0, 0), (0, 0), (0, 0), (0, LANES - ROPE))).reshape(N_B, Q_LORA, H_B * LANES)
    wq_s = jnp.concatenate([uq_nope, pad(uq_rope), pad(uq_rot)], axis=-1).astype(BF16)

    cos_p, sin_p = _rope_tables(0, t_p)
    cos_s, sin_s = _rope_tables(past, t_s)
    tile2 = lambda a: jnp.concatenate([a, a], axis=1)
    cos8t_p, sin8t_p = cos_p[:, :HALF].T, sin_p[:, :HALF].T
    padl = lambda a: jnp.tile(jnp.pad(a, ((0, 0), (0, LANES - ROPE))), (1, H_B))
    cos8_s, sin8_s = padl(cos_s), padl(sin_s)

    def gla_weights(l):
        return (_row(a_norm_pre[l]), a_win[l], a_wglr[l], a_wg2[l], _row(a_b_gate[l]),
                _row(a_head_norm[l]), a_wout[l], _row(a_norm_post[l]))

    assert t_p // CHUNK <= LANES - ROPE
    koh = jax.nn.one_hot(ROPE + jnp.arange(t_p) // CHUNK, LANES, dtype=F32)
    kv_args = (tile2(cos_p), tile2(sin_p), koh, _row(kv_norm), wdc, wdr, _row(kv_latent_norm), wuk, wuvt)
    x, st_p = x_prompt, None
    for l in range(N_A - 1):
        x, st_p = _gla_prompt_layer(x, st_p, gla_weights(l), rows=GLA_ROWS, chunk=CHUNK)
    x, st_p, ckv_p, kr_p, kn, kr0, vt = _gla_prompt_layer(
        x, st_p, gla_weights(N_A - 1), kv_args, rows=GLA_ROWS, chunk=CHUNK, kv_blk=ATT_K)
    for j in range(N_B):
        x = _mla_prompt_layer(x, kn, kr0, vt, cos8t_p, sin8t_p, _row(b_norm_pre[j]), b_win[j],
                              _row(b_q_norm[j]), wqt_p[j], b_wout[j], _row(b_norm_post[j]),
                              tq=ATT_Q, tk=ATT_K)
    y_prompt = x

    assert t_s <= CHUNK
    bsz_s = x_sample.shape[0]
    x, states = x_sample, []
    for l in range(N_A):
        x, s_new = _gla_sample_layer(x, state_gla[l], gla_weights(l), groups=SAMPLE_GROUPS)
        states.append(s_new)
    st_s = jnp.stack(states)
    per_row = lambda a: jnp.tile(tile2(a), (bsz_s, 1))
    ckv_s, kr_s = _shared_kv_latent(x.reshape(bsz_s * t_s, D_MODEL), per_row(cos_s), per_row(sin_s),
                                    _row(kv_norm), wdc, wdr, _row(kv_latent_norm))
    ckv_s = ckv_s.reshape(bsz_s, t_s, KV_LORA)
    kr_s = kr_s.reshape(bsz_s, t_s, ROPE)
    for j in range(N_B):
        x = _mla_sample_layer(x, cache_ckv, cache_krope, ckv_s, kr_s, cos8_s, sin8_s,
                              _row(b_norm_pre[j]), b_win[j], _row(b_q_norm[j]), wq_s[j], wuk, wuv,
                              b_wout[j], _row(b_norm_post[j]))
    y_sample = x

    return (y_prompt, y_sample, st_p, st_s, ckv_p, kr_p, ckv_s, kr_s)
```

```python
import functools

import jax
import jax.numpy as jnp
from jax import lax
from jax.experimental import pallas as pl
from jax.experimental.pallas import tpu as pltpu

F32 = jnp.float32
BF16 = jnp.bfloat16

D_MODEL = 1024
CHUNK = 64
N_A = 2
N_B = 2
H_A = 4
DK_A = 128
DV_A = 256
KD_A = H_A * DK_A
VD_A = H_A * DV_A
GATE_RANK = 16
GATE_TAU = 16.0
H_B = 8
NOPE = 128
ROPE = 64
HALF = ROPE // 2
V_B = 128
VD_B = H_B * V_B
Q_LORA = 384
KV_LORA = 256
ROPE_BASE = 10000.0
EPS = 1e-6

LANES = 128
VMEM_LIMIT = 56 * 1024 * 1024

GLA_ROWS = 512
GLA_SUB = 256
SAMPLE_GROUPS = 2
ATT_Q = 512
ATT_K = 256
MASKED = -1e30


def _dot(a, b):
    return jnp.dot(a, b, preferred_element_type=F32)


def _dot_nt(a, b):
    return lax.dot_general(a, b, (((1,), (1,)), ((), ())), preferred_element_type=F32)


def _dot_tn(a, b):
    return lax.dot_general(a, b, (((0,), (0,)), ((), ())), preferred_element_type=F32)


def _rms(xf, g):
    ms = jnp.mean(xf * xf, axis=-1, keepdims=True)
    return xf * lax.rsqrt(ms + EPS) * g


def _silu(g):
    return g * (1.0 / (1.0 + jnp.exp(-g)))


def _const_spec(shape):
    return pl.BlockSpec(shape, lambda *_: (0,) * len(shape))


class _Layer:
    def __init__(self, stacked, l):
        self.stacked, self.l, self.shape = stacked, l, stacked.shape[1:]


def _weight_spec(w):
    if isinstance(w, _Layer):
        l, nd = w.l, len(w.shape)
        return pl.BlockSpec((None,) + w.shape, lambda *_: (l,) + (0,) * nd)
    return _const_spec(w.shape)


def _weight_arg(w):
    return w.stacked if isinstance(w, _Layer) else w


def _kv_latent(xb, cos, sin, nkv_ref, wdc_ref, wdr_ref, nlat_ref):
    hb = _rms(xb, nkv_ref[...]).astype(BF16)
    ckv = _rms(_dot(hb, wdc_ref[...]), nlat_ref[...])
    rr = _dot(hb, wdr_ref[...])
    return ckv, rr[:, :LANES] * cos + rr[:, LANES:] * sin


def _gla_kernel(*refs, rows, chunk, carry, n_prev, kv_blk):
    refs = list(refs)
    x_ref = refs.pop(0)
    s0_ref = None if carry else refs.pop(0)
    prev_ref = refs.pop(0) if n_prev else None
    npre_ref, win_ref, wg2_ref, bg_ref, hn_ref, wout_ref, npost_ref = refs[:7]
    del refs[:7]
    if kv_blk:
        (cos_ref, sin_ref, koh_ref, nkv_ref, wdc_ref, wdr_ref, nlat_ref, wuk_ref, wuvt_ref) = refs[:9]
        del refs[:9]
    y_ref, sout_ref = refs[:2]
    del refs[:2]
    if kv_blk:
        ckv_ref, kr_ref, kn_ref, kr0_ref, vt_ref = refs[:5]
        del refs[:5]
    st_sc = refs.pop(0) if carry else None
    o_sc, qe_sc, u_sc, stb_sc = refs
    t = pl.program_id(1)

    if carry:
        @pl.when(t == 0)
        def _():
            st_sc[...] = jnp.zeros(st_sc.shape, F32)

    shift = chunk.bit_length() - 1
    sub = min(rows, GLA_SUB)
    blocks = range(0, rows, sub)
    xs, z, proj, b = {}, {}, {}, {}
    n_main = 2 * KD_A + 2 * VD_A
    wglr = jnp.concatenate(
        [win_ref[:, n_main:], jnp.zeros((win_ref.shape[0], 2 * LANES - GATE_RANK), BF16)], axis=1)
    for r in blocks:
        xs[r] = x_ref[0, r:r + sub, :]
        hb = _rms(xs[r], npre_ref[...]).astype(BF16)
        glr = _dot(hb, wglr)[:, :GATE_RANK].astype(BF16)
        z[r] = _dot(glr, wg2_ref[...]) + bg_ref[...]
        proj[r] = _dot(hb, win_ref[:, :n_main])

    ri = lax.broadcasted_iota(jnp.int32, (sub, sub), 0)
    ci = lax.broadcasted_iota(jnp.int32, (sub, sub), 1)
    inchunk = jnp.where(jnp.right_shift(ri, shift) == jnp.right_shift(ci, shift), ci, sub) <= ri
    tri = jnp.where(inchunk, 1.0, 0.0).astype(BF16)
    for r in blocks:
        g = (jnp.minimum(z[r], 0.0) - jnp.log(1.0 + jnp.exp(-jnp.abs(z[r])))) * (1.0 / GATE_TAU)
        g_hi = g.astype(BF16)
        g_lo = (g - g_hi.astype(F32)).astype(BF16)
        b[r] = _dot(tri, g_hi) + _dot(tri, g_lo)

    scale = DK_A ** -0.5
    nc = rows // chunk
    decay = [[None] * nc for _ in range(H_A)]

    for h in range(H_A):
        for r in blocks:
            chunks = range(r // chunk, (r + sub) // chunk)
            bh = b[r][:, h * DK_A:(h + 1) * DK_A]
            ends = {c: bh[(c + 1) * chunk - 1 - r:(c + 1) * chunk - r, :] for c in chunks}
            bl = jnp.concatenate([jnp.broadcast_to(ends[c], (chunk, DK_A)) for c in chunks], axis=0)
            qh = proj[r][:, h * DK_A:(h + 1) * DK_A]
            kh = proj[r][:, KD_A + h * DK_A:KD_A + (h + 1) * DK_A]
            vh = proj[r][:, 2 * KD_A + h * DV_A:2 * KD_A + (h + 1) * DV_A].astype(BF16)
            qe = (qh * scale * jnp.exp(bh)).astype(BF16)
            ke = (kh * jnp.exp(-bh)).astype(BF16)
            kd = (kh * jnp.exp(bl - bh)).astype(BF16)
            s = jnp.where(inchunk, _dot_nt(qe, ke), 0.0).astype(BF16)
            o_sc[r:r + sub, h * DV_A:(h + 1) * DV_A] = _dot(s, vh)
            qe_sc[r:r + sub, h * DK_A:(h + 1) * DK_A] = qe
            for c in chunks:
                r0, r1 = c * chunk - r, (c + 1) * chunk - r
                u_sc[h, c] = _dot_tn(kd[r0:r1], vh[r0:r1])
                dcol = jnp.broadcast_to(jnp.exp(ends[c]), (LANES, DK_A)).T
                decay[h][c] = jnp.concatenate([dcol] * (DV_A // LANES), axis=1)

    for h in range(H_A):
        if carry:
            st = st_sc[h]
        for c in range(nc):
            if not carry:
                st = s0_ref[c, h]
            stb_sc[h, c] = st.astype(BF16)
            st_new = st * decay[h][c] + u_sc[h, c]
            if carry:
                st = st_new
            else:
                sout_ref[c, h] = st_new
        if carry:
            st_sc[h] = st

    for c in range(nc):
        r0, r1 = c * chunk, (c + 1) * chunk
        for h in range(H_A):
            o_sc[r0:r1, h * DV_A:(h + 1) * DV_A] += _dot(qe_sc[r0:r1, h * DK_A:(h + 1) * DK_A], stb_sc[h, c])

    for r in blocks:
        gate = proj[r][:, 2 * KD_A + VD_A:2 * KD_A + 2 * VD_A]
        heads = []
        for h in range(H_A):
            heads.append(_rms(o_sc[r:r + sub, h * DV_A:(h + 1) * DV_A], hn_ref[:, h * DV_A:(h + 1) * DV_A]))
        og = (jnp.concatenate(heads, axis=1) * _silu(gate)).astype(BF16)
        y = _dot(og, wout_ref[...])
        x_new = xs[r] + _rms(y, npost_ref[...])
        y_ref[0, r:r + sub, :] = x_new
        if kv_blk:
            ckv, kr2 = _kv_latent(x_new, cos_ref[r:r + sub, :], sin_ref[r:r + sub, :],
                                  nkv_ref, wdc_ref, wdr_ref, nlat_ref)
            ckv_ref[0, r:r + sub, :] = ckv
            kr_ref[0, r:r + sub, :] = kr2[:, :ROPE]
            cb = ckv.astype(BF16)
            kn_ref[0, r:r + sub, :] = _dot(cb, wuk_ref[...]).astype(BF16)
            lane = lax.broadcasted_iota(jnp.int32, kr2.shape, 1)
            kr0_ref[0, r:r + sub, :] = jnp.where(lane < ROPE, kr2, koh_ref[r:r + sub, :]).astype(BF16)
            vt = _dot_nt(wuvt_ref[...], cb).astype(BF16)
            for i in range(sub // kv_blk):
                vt_ref[0, r // kv_blk + i] = vt[:, i * kv_blk:(i + 1) * kv_blk]

    if carry:
        @pl.when(t == pl.num_programs(1) - 1)
        def _():
            for li in range(n_prev):
                sout_ref[li, 0] = prev_ref[li, 0]
            for h in range(H_A):
                sout_ref[n_prev, 0, h] = st_sc[h]


def _gla_scratch(rows, chunk):
    return [pltpu.VMEM((rows, VD_A), F32), pltpu.VMEM((rows, KD_A), BF16),
            pltpu.VMEM((H_A, rows // chunk, DK_A, DV_A), F32),
            pltpu.VMEM((H_A, rows // chunk, DK_A, DV_A), BF16)]


def _gla_prompt_layer(x, prev, weights, kv=None, *, rows, chunk, kv_blk=None):
    bsz, tlen, d = x.shape
    n_prev = 0 if prev is None else prev.shape[0]
    tok = lambda w: pl.BlockSpec((1, rows, w), lambda b, t: (b, t, 0))
    st_spec = lambda n: pl.BlockSpec((n, 1, H_A, DK_A, DV_A), lambda b, t: (0, b, 0, 0, 0))
    in_specs = [tok(d)] + ([st_spec(n_prev)] if n_prev else []) + [_weight_spec(a) for a in weights]
    args = [x] + ([prev] if n_prev else []) + [_weight_arg(a) for a in weights]
    out_specs = [tok(d), st_spec(n_prev + 1)]
    out_shape = [jax.ShapeDtypeStruct((bsz, tlen, d), F32),
                 jax.ShapeDtypeStruct((n_prev + 1, bsz, H_A, DK_A, DV_A), F32)]
    if kv is not None:
        assert min(rows, GLA_SUB) % kv_blk == 0
        tables, kv_weights = kv[:3], kv[3:]
        tab = pl.BlockSpec((rows, LANES), lambda b, t: (t, 0))
        in_specs += [tab] * 3 + [_const_spec(a.shape) for a in kv_weights]
        args += list(tables) + list(kv_weights)
        out_specs += [tok(KV_LORA), tok(ROPE), tok(H_B * NOPE), tok(LANES),
                      pl.BlockSpec((1, rows // kv_blk, VD_B, kv_blk), lambda b, t: (b, t, 0, 0))]
        out_shape += [jax.ShapeDtypeStruct((bsz, tlen, KV_LORA), F32),
                      jax.ShapeDtypeStruct((bsz, tlen, ROPE), F32),
                      jax.ShapeDtypeStruct((bsz, tlen, H_B * NOPE), BF16),
                      jax.ShapeDtypeStruct((bsz, tlen, LANES), BF16),
                      jax.ShapeDtypeStruct((bsz, tlen // kv_blk, VD_B, kv_blk), BF16)]
    return pl.pallas_call(
        functools.partial(_gla_kernel, rows=rows, chunk=chunk, carry=True, n_prev=n_prev,
                          kv_blk=kv_blk if kv is not None else None),
        grid=(bsz, tlen // rows),
        in_specs=in_specs,
        out_specs=tuple(out_specs),
        out_shape=tuple(out_shape),
        scratch_shapes=[pltpu.VMEM((H_A, DK_A, DV_A), F32)] + _gla_scratch(rows, chunk),
        compiler_params=pltpu.CompilerParams(
            dimension_semantics=("parallel", "arbitrary"), vmem_limit_bytes=VMEM_LIMIT),
        name="gla_layer",
    )(*args)


def _gla_sample_layer(x, states, layer, weights, *, groups):
    bsz, t, d = x.shape
    per = bsz // groups
    rows = per * t
    tok = pl.BlockSpec((1, rows, d), lambda g, _: (g, 0, 0))
    st_spec = pl.BlockSpec((per, H_A, DK_A, DV_A), lambda g, _: (g, 0, 0, 0))
    st_in = pl.BlockSpec((None, per, H_A, DK_A, DV_A), lambda g, _: (layer, g, 0, 0, 0))
    y, s_new = pl.pallas_call(
        functools.partial(_gla_kernel, rows=rows, chunk=t, carry=False, n_prev=0, kv_blk=None),
        grid=(groups, 1),
        in_specs=[tok, st_in] + [_weight_spec(a) for a in weights],
        out_specs=(tok, st_spec),
        out_shape=(jax.ShapeDtypeStruct((groups, rows, d), F32),
                   jax.ShapeDtypeStruct((bsz, H_A, DK_A, DV_A), F32)),
        scratch_shapes=_gla_scratch(rows, t),
        compiler_params=pltpu.CompilerParams(
            dimension_semantics=("parallel", "arbitrary"), vmem_limit_bytes=VMEM_LIMIT),
        name="gla_sample_layer",
    )(x.reshape(groups, rows, d), states, *[_weight_arg(a) for a in weights])
    return y.reshape(bsz, t, d), s_new


def _kv_kernel(x_ref, cos_ref, sin_ref, nkv_ref, wdc_ref, wdr_ref, nlat_ref, ckv_ref, kr_ref):
    ckv, kr2 = _kv_latent(x_ref[...], cos_ref[...], sin_ref[...], nkv_ref, wdc_ref, wdr_ref, nlat_ref)
    ckv_ref[...] = ckv
    kr_ref[...] = kr2[:, :ROPE]


def _shared_kv_latent(x, cos2, sin2, nkv, wdc, wdr, nlat):
    n, d = x.shape
    args = [x, cos2, sin2, nkv, wdc, wdr, nlat]
    return pl.pallas_call(
        _kv_kernel,
        grid=(1,),
        in_specs=[_const_spec(a.shape) for a in args],
        out_specs=(_const_spec((n, KV_LORA)), _const_spec((n, ROPE))),
        out_shape=(jax.ShapeDtypeStruct((n, KV_LORA), F32), jax.ShapeDtypeStruct((n, ROPE), F32)),
        compiler_params=pltpu.CompilerParams(
            dimension_semantics=("arbitrary",), vmem_limit_bytes=VMEM_LIMIT),
        name="shared_kv",
    )(*args)


def _mla_q(x, cos, sin, npre_ref, win_ref, qn_ref, wq_ref, rope_w):
    hb = _rms(x, npre_ref[...]).astype(BF16)
    proj = _dot(hb, win_ref[...])
    qn = _rms(proj[:, :Q_LORA], qn_ref[...]).astype(BF16)
    qq = _dot(qn, wq_ref[...])
    scale = (NOPE + ROPE) ** -0.5
    n0, n1 = H_B * NOPE, H_B * NOPE + H_B * rope_w
    q_nope = (qq[:, :n0] * scale).astype(BF16)
    q_rope = ((qq[:, n0:n1] * cos + qq[:, n1:] * sin) * scale).astype(BF16)
    return q_nope, q_rope, proj[:, Q_LORA:]


ACC_ROWS = V_B + 16


def _mla_kernel(x_ref, kn_ref, kr_ref, vt_ref, cos_ref, sin_ref, npre_ref, win_ref, qn_ref, wqt_ref,
                wout_ref, npost_ref, y_ref, q_sc, m_sc, acc_sc, alpha_sc, p_sc, *, tq, tk):
    i = pl.program_id(1)
    c = (NOPE + ROPE) ** -0.5 * 1.4426950408889634
    n0, n1 = H_B * NOPE, H_B * (NOPE + ROPE)
    x = x_ref[0]
    hb = _rms(x, npre_ref[...]).astype(BF16)
    qlat = _dot(hb, win_ref[:, :Q_LORA])
    qnt = _rms(qlat, qn_ref[...]).T.astype(BF16)
    qqt = _dot(wqt_ref[...], qnt)
    q_sc[:n0, :] = (qqt[:n0] * c).astype(BF16)
    cos, sin = cos_ref[...], sin_ref[...]
    for h in range(H_B):
        r0 = n0 + h * ROPE
        x1, x2 = qqt[r0:r0 + HALF], qqt[r0 + HALF:r0 + ROPE]
        q_sc[r0:r0 + HALF, :] = ((x1 * cos - x2 * sin) * c).astype(BF16)
        q_sc[r0 + HALF:r0 + ROPE, :] = ((x2 * cos + x1 * sin) * c).astype(BF16)
    row = lax.broadcasted_iota(jnp.int32, (ROPE, tq), 0)
    q_chunk = jnp.right_shift(i * tq + lax.broadcasted_iota(jnp.int32, (ROPE, tq), 1), CHUNK.bit_length() - 1)
    q_sc[n1:, :] = jnp.where(row > q_chunk, MASKED, 0.0).astype(BF16)
    m_sc[...] = jnp.full(m_sc.shape, -jnp.inf, F32)
    acc_sc[...] = jnp.zeros(acc_sc.shape, F32)

    ones = jnp.ones((ACC_ROWS - V_B, tk), BF16)

    def scores(j, h, lo, hi):
        k0 = pl.multiple_of(j * tk, tk)
        kb = jnp.concatenate(
            [kn_ref[0, pl.ds(k0, tk), h * NOPE:(h + 1) * NOPE], kr_ref[0, pl.ds(k0, tk), :]], axis=1)
        qt = jnp.concatenate(
            [q_sc[h * NOPE:(h + 1) * NOPE, lo:hi], q_sc[n0 + h * ROPE:n0 + (h + 1) * ROPE, lo:hi],
             q_sc[n1:, lo:hi]], axis=0)
        return _dot(kb, qt)

    def softmax(s, h, slot, lo, hi):
        m_prev = m_sc[h, :, lo:hi]
        m_new = jnp.maximum(m_prev, jnp.max(s, axis=0, keepdims=True))
        alpha_sc[slot, h, :, lo:hi] = jnp.exp2(m_prev - m_new)
        p_sc[slot, h, :, lo:hi] = jnp.exp2(s - m_new).astype(BF16)
        m_sc[h, :, lo:hi] = m_new

    def values(j, h, slot, lo, hi):
        vt = jnp.concatenate([vt_ref[0, j, h * V_B:(h + 1) * V_B, :], ones], axis=0)
        acc_sc[h, :, lo:hi] = (alpha_sc[slot, h, :, lo:hi] * acc_sc[h, :, lo:hi]
                               + _dot(vt, p_sc[slot, h, :, lo:hi]))

    for h in range(H_B):
        softmax(scores(0, h, 0, tq), h, 0, 0, tq)

    def block(j, slot_new, slot_old, lo=0):
        s = [scores(j, h, lo, tq) for h in range(H_B)]
        for h in range(H_B):
            values(j - 1, h, slot_old, 0, tq)
        for h in range(H_B):
            softmax(s[h], h, slot_new, lo, tq)

    def pair(jj, carry):
        block(2 * jj + 1, 1, 0)
        block(2 * jj + 2, 0, 1)
        return carry

    def tail_odd(last, lo=0):
        block(last, 1, 0, lo)
        for h in range(H_B):
            values(last, h, 1, lo, tq)

    def tail_even(last):
        for h in range(H_B):
            values(last, h, 0, 0, tq)

    if tq == 2 * tk:
        lax.fori_loop(0, i, pair, 0)
        tail_odd(2 * i + 1, tk)
    else:
        lax.fori_loop(0, i // 2, pair, 0)
        odd = lax.rem(i, 2) == 1
        pl.when(odd)(lambda: tail_odd(i))
        pl.when(jnp.logical_not(odd))(lambda: tail_even(i))

    for r in range(0, tq, tk):
        gate = _dot(hb[r:r + tk], win_ref[:, Q_LORA:])
        outs = []
        for h in range(H_B):
            a = acc_sc[h, :, r:r + tk]
            outs.append(a[:V_B] * (1.0 / a[V_B:V_B + 1]))
        o = jnp.concatenate(outs, axis=0).T
        og = (o * _silu(gate)).astype(BF16)
        y = _dot(og, wout_ref[...])
        y_ref[0, r:r + tk, :] = x[r:r + tk] + _rms(y, npost_ref[...])


def _mla_prompt_layer(x, kn, kr0, vt, cos8t, sin8t, npre, win, qn, wqt, wout, npost, *, tq, tk):
    bsz, tlen, d = x.shape
    assert tq in (tk, 2 * tk) and tk % CHUNK == 0 and vt.shape[3] == tk
    tok = pl.BlockSpec((1, tq, d), lambda b, t: (b, t, 0))
    full = lambda w: pl.BlockSpec((1, tlen, w), lambda b, t: (b, 0, 0))
    tab = pl.BlockSpec((cos8t.shape[0], tq), lambda b, t: (0, t))
    weights = [npre, win, qn, wqt, wout, npost]
    return pl.pallas_call(
        functools.partial(_mla_kernel, tq=tq, tk=tk),
        grid=(bsz, tlen // tq),
        in_specs=[tok, full(kn.shape[2]), full(kr0.shape[2]),
                  pl.BlockSpec((1,) + vt.shape[1:], lambda b, t: (b, 0, 0, 0)), tab, tab]
                 + [_weight_spec(a) for a in weights],
        out_specs=tok,
        out_shape=jax.ShapeDtypeStruct((bsz, tlen, d), F32),
        scratch_shapes=[pltpu.VMEM((H_B * (NOPE + ROPE) + ROPE, tq), BF16),
                        pltpu.VMEM((H_B, 1, tq), F32),
                        pltpu.VMEM((H_B, ACC_ROWS, tq), F32),
                        pltpu.VMEM((2, H_B, 1, tq), F32),
                        pltpu.VMEM((2, H_B, tk, tq), BF16)],
        compiler_params=pltpu.CompilerParams(
            dimension_semantics=("parallel", "arbitrary"), vmem_limit_bytes=VMEM_LIMIT),
        name="mla_prompt_layer",
    )(x, kn, kr0, vt, cos8t, sin8t, *[_weight_arg(a) for a in weights])


def _mla_sample_kernel(x_ref, cckv_ref, ckr_ref, nckv_ref, nkr_ref, cos_ref, sin_ref, npre_ref, win_ref,
                       qn_ref, wq_ref, wuk_ref, wuv_ref, wout_ref, npost_ref, y_ref, *, past):
    x = x_ref[0]
    t = x.shape[0]
    q_nope, q_rope, gate = _mla_q(x, cos_ref[...], sin_ref[...], npre_ref, win_ref, qn_ref, wq_ref, LANES)

    qlat = jnp.concatenate(
        [_dot_nt(q_nope[:, h * NOPE:(h + 1) * NOPE], wuk_ref[:, h * NOPE:(h + 1) * NOPE]) for h in range(H_B)],
        axis=0).astype(BF16)
    qr = jnp.concatenate([q_rope[:, h * LANES:h * LANES + ROPE] for h in range(H_B)], axis=0)

    c_ckv = cckv_ref[0].astype(BF16)
    n_ckv = nckv_ref[0].astype(BF16)
    s_c = _dot_nt(qlat, c_ckv) + _dot_nt(qr, ckr_ref[0].astype(BF16))
    s_n = _dot_nt(qlat, n_ckv) + _dot_nt(qr, nkr_ref[0].astype(BF16))

    cshift = CHUNK.bit_length() - 1

    def chunk_mask(s, k_start):
        rows = lax.broadcasted_iota(jnp.int32, s.shape, 0)
        cols = lax.broadcasted_iota(jnp.int32, s.shape, 1)
        q_chunk = jnp.right_shift(past + jnp.bitwise_and(rows, t - 1), cshift)
        k_chunk = jnp.right_shift(k_start + cols, cshift)
        return jnp.where(k_chunk <= q_chunk, s, -jnp.inf)

    s_c = chunk_mask(s_c, 0)
    s_n = chunk_mask(s_n, past)
    m = jnp.maximum(jnp.max(s_c, axis=-1, keepdims=True), jnp.max(s_n, axis=-1, keepdims=True))
    p_c = jnp.exp(s_c - m)
    p_n = jnp.exp(s_n - m)
    l = jnp.sum(p_c, axis=-1, keepdims=True) + jnp.sum(p_n, axis=-1, keepdims=True)
    olat = (_dot(p_c.astype(BF16), c_ckv) + _dot(p_n.astype(BF16), n_ckv)) * (1.0 / l)
    olat = olat.astype(BF16)
    o = jnp.concatenate(
        [_dot(olat[h * t:(h + 1) * t], wuv_ref[:, h * V_B:(h + 1) * V_B]) for h in range(H_B)], axis=1)
    og = (o * _silu(gate)).astype(BF16)
    y = _dot(og, wout_ref[...])
    y_ref[0] = x + _rms(y, npost_ref[...])


def _mla_sample_layer(x, cache_ckv, cache_kr, new_ckv, new_kr, cos, sin, npre, win, qn, wq, wuk, wuv,
                      wout, npost):
    bsz, t, d = x.shape
    past = cache_ckv.shape[1]
    assert t & (t - 1) == 0
    per_b = lambda a: pl.BlockSpec((1,) + a.shape[1:], lambda b: (b, 0, 0))
    consts = [cos, sin, npre, win, qn, wq, wuk, wuv, wout, npost]
    return pl.pallas_call(
        functools.partial(_mla_sample_kernel, past=past),
        grid=(bsz,),
        in_specs=[per_b(x), per_b(cache_ckv), per_b(cache_kr), per_b(new_ckv), per_b(new_kr)]
                 + [_weight_spec(a) for a in consts],
        out_specs=per_b(x),
        out_shape=jax.ShapeDtypeStruct((bsz, t, d), F32),
        compiler_params=pltpu.CompilerParams(
            dimension_semantics=("parallel",), vmem_limit_bytes=VMEM_LIMIT),
        name="mla_sample_layer",
    )(x, cache_ckv, cache_kr, new_ckv, new_kr, *[_weight_arg(a) for a in consts])


def _rope_tables(pos_offset, tlen):
    inv = jnp.power(ROPE_BASE, -jnp.arange(HALF, dtype=F32) / HALF)
    ang = (pos_offset + jnp.arange(tlen)).astype(F32)[:, None] * inv[None, :]
    cos = jnp.cos(ang)
    sin = jnp.sin(ang)
    return jnp.concatenate([cos, cos], axis=1), jnp.concatenate([sin, sin], axis=1)


def _rot_cols(w):
    return jnp.concatenate([-w[..., HALF:], w[..., :HALF]], axis=-1)


def _row(a):
    return a.reshape(1, -1)


def kernel(x_prompt, x_sample, state_gla, cache_ckv, cache_krope, a_norm_pre, a_w_in, a_w_gate2, a_b_gate,
           a_head_norm, a_w_out, a_norm_post, kv_norm, kv_w_down, kv_latent_norm, kv_w_uk, kv_w_uv,
           b_norm_pre, b_w_in, b_q_norm, b_w_uq, b_w_out, b_norm_post):
    past = cache_ckv.shape[1]
    t_p, t_s = x_prompt.shape[1], x_sample.shape[1]

    a_win = a_w_in.astype(BF16)
    a_wg2 = a_w_gate2.astype(BF16)
    a_wout = a_w_out.astype(BF16)
    wdc = kv_w_down[:, :KV_LORA].astype(BF16)
    wdr_cols = kv_w_down[:, KV_LORA:]
    wdr = jnp.concatenate([wdr_cols, wdr_cols, _rot_cols(wdr_cols), _rot_cols(wdr_cols)], axis=1).astype(BF16)
    wuk = kv_w_uk.astype(BF16)
    wuv = kv_w_uv.astype(BF16)
    b_win = b_w_in.astype(BF16)
    b_wout = b_w_out.astype(BF16)
    uq = b_w_uq.reshape(N_B, Q_LORA, H_B, NOPE + ROPE)
    uq_nope = uq[..., :NOPE].reshape(N_B, Q_LORA, H_B * NOPE)
    uq_rope = uq[..., NOPE:]
    uq_rot = _rot_cols(uq_rope)
    wqt_p = jnp.concatenate([uq_nope, uq_rope.reshape(N_B, Q_LORA, H_B * ROPE)],
                            axis=-1).astype(BF16).transpose(0, 2, 1)
    wuvt = wuv.T
    pad = lambda w: jnp.pad(w, ((0, 0), (0, 0), (0, 0), (0, LANES - ROPE))).reshape(N_B, Q_LORA, H_B * LANES)
    wq_s = jnp.concatenate([uq_nope, pad(uq_rope), pad(uq_rot)], axis=-1).astype(BF16)

    cos_p, sin_p = _rope_tables(0, t_p)
    cos_s, sin_s = _rope_tables(past, t_s)
    tile2 = lambda a: jnp.concatenate([a, a], axis=1)
    cos8t_p, sin8t_p = cos_p[:, :HALF].T, sin_p[:, :HALF].T
    padl = lambda a: jnp.tile(jnp.pad(a, ((0, 0), (0, LANES - ROPE))), (1, H_B))
    cos8_s, sin8_s = padl(cos_s), padl(sin_s)

    def lay(a, l):
        return _Layer(a.reshape(a.shape[0], 1, -1) if a.ndim == 2 else a, l)

    def gla_weights(l):
        return (lay(a_norm_pre, l), lay(a_win, l), lay(a_wg2, l), lay(a_b_gate, l),
                lay(a_head_norm, l), lay(a_wout, l), lay(a_norm_post, l))

    assert t_p // CHUNK <= LANES - ROPE
    koh = jax.nn.one_hot(ROPE + jnp.arange(t_p) // CHUNK, LANES, dtype=F32)
    kv_args = (tile2(cos_p), tile2(sin_p), koh, _row(kv_norm), wdc, wdr, _row(kv_latent_norm), wuk, wuvt)
    x, st_p = x_prompt, None
    for l in range(N_A - 1):
        x, st_p = _gla_prompt_layer(x, st_p, gla_weights(l), rows=GLA_ROWS, chunk=CHUNK)
    x, st_p, ckv_p, kr_p, kn, kr0, vt = _gla_prompt_layer(
        x, st_p, gla_weights(N_A - 1), kv_args, rows=GLA_ROWS, chunk=CHUNK, kv_blk=ATT_K)
    for j in range(N_B):
        x = _mla_prompt_layer(x, kn, kr0, vt, cos8t_p, sin8t_p, lay(b_norm_pre, j), lay(b_win, j),
                              lay(b_q_norm, j), lay(wqt_p, j), lay(b_wout, j), lay(b_norm_post, j),
                              tq=ATT_Q, tk=ATT_K)
    y_prompt = x

    assert t_s <= CHUNK
    bsz_s = x_sample.shape[0]
    x, states = x_sample, []
    for l in range(N_A):
        x, s_new = _gla_sample_layer(x, state_gla, l, gla_weights(l), groups=SAMPLE_GROUPS)
        states.append(s_new)
    st_s = jnp.stack(states)
    per_row = lambda a: jnp.tile(tile2(a), (bsz_s, 1))
    ckv_s, kr_s = _shared_kv_latent(x.reshape(bsz_s * t_s, D_MODEL), per_row(cos_s), per_row(sin_s),
                                    _row(kv_norm), wdc, wdr, _row(kv_latent_norm))
    ckv_s = ckv_s.reshape(bsz_s, t_s, KV_LORA)
    kr_s = kr_s.reshape(bsz_s, t_s, ROPE)
    for j in range(N_B):
        x = _mla_sample_layer(x, cache_ckv, cache_krope, ckv_s, kr_s, cos8_s, sin8_s,
                              lay(b_norm_pre, j), lay(b_win, j), lay(b_q_norm, j), lay(wq_s, j), wuk, wuv,
                              lay(b_wout, j), lay(b_norm_post, j))
    y_sample = x

    return (y_prompt, y_sample, st_p, st_s, ckv_p, kr_p, ckv_s, kr_s)
```

```python
import functools

import jax
import jax.numpy as jnp
from jax import lax
from jax.experimental import pallas as pl
from jax.experimental.pallas import tpu as pltpu

F32 = jnp.float32
BF16 = jnp.bfloat16

D_MODEL = 1024
CHUNK = 64
N_A = 2
N_B = 2
H_A = 4
DK_A = 128
DV_A = 256
KD_A = H_A * DK_A
VD_A = H_A * DV_A
GATE_RANK = 16
GATE_TAU = 16.0
H_B = 8
NOPE = 128
ROPE = 64
HALF = ROPE // 2
V_B = 128
VD_B = H_B * V_B
Q_LORA = 384
KV_LORA = 256
ROPE_BASE = 10000.0
EPS = 1e-6

LANES = 128
VMEM_LIMIT = 56 * 1024 * 1024

GLA_ROWS = 512
GLA_SUB = 256
SAMPLE_GROUPS = 2
ATT_Q = 512
ATT_K = 256
MASKED = -1e30


def _dot(a, b):
    return jnp.dot(a, b, preferred_element_type=F32)


def _dot_nt(a, b):
    return lax.dot_general(a, b, (((1,), (1,)), ((), ())), preferred_element_type=F32)


def _dot_tn(a, b):
    return lax.dot_general(a, b, (((0,), (0,)), ((), ())), preferred_element_type=F32)


def _rms(xf, g):
    ms = jnp.mean(xf * xf, axis=-1, keepdims=True)
    return xf * lax.rsqrt(ms + EPS) * g


def _silu(g):
    return g * (1.0 / (1.0 + jnp.exp(-g)))


def _const_spec(shape):
    return pl.BlockSpec(shape, lambda *_: (0,) * len(shape))


class _Layer:
    def __init__(self, stacked, l):
        self.stacked, self.l, self.shape = stacked, l, stacked.shape[1:]


def _weight_spec(w):
    if isinstance(w, _Layer):
        l, nd = w.l, len(w.shape)
        return pl.BlockSpec((None,) + w.shape, lambda *_: (l,) + (0,) * nd)
    return _const_spec(w.shape)


def _weight_arg(w):
    return w.stacked if isinstance(w, _Layer) else w


def _kv_down(xb, nkv_ref, wdc_ref, wdr_ref):
    hb = _rms(xb, nkv_ref[...]).astype(BF16)
    return _dot(hb, wdc_ref[...]), _dot(hb, wdr_ref[...])


def _kv_finish(lat, rr, cos, sin, nlat_ref):
    return _rms(lat, nlat_ref[...]), rr[:, :LANES] * cos + rr[:, LANES:] * sin


def _gla_kernel(*refs, rows, chunk, carry, n_prev, kv_blk):
    refs = list(refs)
    x_ref = refs.pop(0)
    s0_ref = None if carry else refs.pop(0)
    prev_ref = refs.pop(0) if n_prev else None
    npre_ref, win_ref, wglr_ref, wg2_ref, bg_ref, hn_ref, wout_ref, npost_ref = refs[:8]
    del refs[:8]
    if kv_blk:
        (cos_ref, sin_ref, koh_ref, nkv_ref, wdc_ref, wdr_ref, nlat_ref, wuk_ref, wuvt_ref) = refs[:9]
        del refs[:9]
    y_ref, sout_ref = refs[:2]
    del refs[:2]
    if kv_blk:
        ckv_ref, kr_ref, kn_ref, kr0_ref, vt_ref = refs[:5]
        del refs[:5]
    st_sc = refs.pop(0) if carry else None
    o_sc, qe_sc, u_sc, stb_sc = refs
    t = pl.program_id(1)

    if carry:
        @pl.when(t == 0)
        def _():
            st_sc[...] = jnp.zeros(st_sc.shape, F32)

    shift = chunk.bit_length() - 1
    sub = min(rows, GLA_SUB)
    blocks = range(0, rows, sub)
    xs, z, proj, b = {}, {}, {}, {}
    n_main = 2 * KD_A + 2 * VD_A
    for r in blocks:
        xs[r] = x_ref[0, r:r + sub, :]
        hb = _rms(xs[r], npre_ref[...]).astype(BF16)
        glr = _dot(hb, wglr_ref[...])[:, :GATE_RANK].astype(BF16)
        proj_qk = _dot(hb, win_ref[:, :2 * KD_A])
        z[r] = _dot(glr, wg2_ref[...]) + bg_ref[...]
        proj[r] = jnp.concatenate([proj_qk, _dot(hb, win_ref[:, 2 * KD_A:n_main])], axis=1)

    ri = lax.broadcasted_iota(jnp.int32, (sub, sub), 0)
    ci = lax.broadcasted_iota(jnp.int32, (sub, sub), 1)
    inchunk = jnp.where(jnp.right_shift(ri, shift) == jnp.right_shift(ci, shift), ci, sub) <= ri
    tri = jnp.where(inchunk, 1.0, 0.0).astype(BF16)
    for r in blocks:
        g = (jnp.minimum(z[r], 0.0) - jnp.log(1.0 + jnp.exp(-jnp.abs(z[r])))) * (1.0 / GATE_TAU)
        g_hi = g.astype(BF16)
        g_lo = (g - g_hi.astype(F32)).astype(BF16)
        b[r] = _dot(tri, g_hi) + _dot(tri, g_lo)

    scale = DK_A ** -0.5
    nc = rows // chunk
    decay = [[None] * nc for _ in range(H_A)]

    for h in range(H_A):
        for r in blocks:
            chunks = range(r // chunk, (r + sub) // chunk)
            bh = b[r][:, h * DK_A:(h + 1) * DK_A]
            ends = {c: bh[(c + 1) * chunk - 1 - r:(c + 1) * chunk - r, :] for c in chunks}
            bl = jnp.concatenate([jnp.broadcast_to(ends[c], (chunk, DK_A)) for c in chunks], axis=0)
            qh = proj[r][:, h * DK_A:(h + 1) * DK_A]
            kh = proj[r][:, KD_A + h * DK_A:KD_A + (h + 1) * DK_A]
            vh = proj[r][:, 2 * KD_A + h * DV_A:2 * KD_A + (h + 1) * DV_A].astype(BF16)
            qe = (qh * scale * jnp.exp(bh)).astype(BF16)
            ke = (kh * jnp.exp(-bh)).astype(BF16)
            kd = (kh * jnp.exp(bl - bh)).astype(BF16)
            s_raw = _dot_nt(qe, ke)
            qe_sc[r:r + sub, h * DK_A:(h + 1) * DK_A] = qe
            for c in chunks:
                r0, r1 = c * chunk - r, (c + 1) * chunk - r
                u_sc[h, c] = _dot_tn(kd[r0:r1], vh[r0:r1])
                dcol = jnp.broadcast_to(jnp.exp(ends[c]), (LANES, DK_A)).T
                decay[h][c] = jnp.concatenate([dcol] * (DV_A // LANES), axis=1)
            s = jnp.where(inchunk, s_raw, 0.0).astype(BF16)
            o_sc[r:r + sub, h * DV_A:(h + 1) * DV_A] = _dot(s, vh)

    for h in range(H_A):
        if carry:
            st = st_sc[h]
        for c in range(nc):
            if not carry:
                st = s0_ref[c, h]
            stb_sc[h, c] = st.astype(BF16)
            st_new = st * decay[h][c] + u_sc[h, c]
            if carry:
                st = st_new
            else:
                sout_ref[c, h] = st_new
        if carry:
            st_sc[h] = st
        for c in range(nc):
            r0, r1 = c * chunk, (c + 1) * chunk
            o_sc[r0:r1, h * DV_A:(h + 1) * DV_A] += _dot(qe_sc[r0:r1, h * DK_A:(h + 1) * DK_A], stb_sc[h, c])

    x_new = {}
    for r in blocks:
        gate = proj[r][:, 2 * KD_A + VD_A:2 * KD_A + 2 * VD_A]
        heads = []
        for h in range(H_A):
            heads.append(_rms(o_sc[r:r + sub, h * DV_A:(h + 1) * DV_A], hn_ref[:, h * DV_A:(h + 1) * DV_A]))
        og = (jnp.concatenate(heads, axis=1) * _silu(gate)).astype(BF16)
        y = _dot(og, wout_ref[...])
        x_new[r] = xs[r] + _rms(y, npost_ref[...])
        y_ref[0, r:r + sub, :] = x_new[r]

    if kv_blk:
        down = {r: _kv_down(x_new[r], nkv_ref, wdc_ref, wdr_ref) for r in blocks}
        for r in blocks:
            ckv, kr2 = _kv_finish(*down[r], cos_ref[r:r + sub, :], sin_ref[r:r + sub, :], nlat_ref)
            ckv_ref[0, r:r + sub, :] = ckv
            kr_ref[0, r:r + sub, :] = kr2[:, :ROPE]
            cb = ckv.astype(BF16)
            kn_ref[0, r:r + sub, :] = _dot(cb, wuk_ref[...]).astype(BF16)
            lane = lax.broadcasted_iota(jnp.int32, kr2.shape, 1)
            kr0_ref[0, r:r + sub, :] = jnp.where(lane < ROPE, kr2, koh_ref[r:r + sub, :]).astype(BF16)
            vt = _dot_nt(wuvt_ref[...], cb).astype(BF16)
            for i in range(sub // kv_blk):
                vt_ref[0, r // kv_blk + i] = vt[:, i * kv_blk:(i + 1) * kv_blk]

    if carry:
        @pl.when(t == pl.num_programs(1) - 1)
        def _():
            for li in range(n_prev):
                sout_ref[li, 0] = prev_ref[li, 0]
            for h in range(H_A):
                sout_ref[n_prev, 0, h] = st_sc[h]


def _gla_scratch(rows, chunk):
    return [pltpu.VMEM((rows, VD_A), F32), pltpu.VMEM((rows, KD_A), BF16),
            pltpu.VMEM((H_A, rows // chunk, DK_A, DV_A), F32),
            pltpu.VMEM((H_A, rows // chunk, DK_A, DV_A), BF16)]


def _gla_prompt_layer(x, prev, weights, kv=None, *, rows, chunk, kv_blk=None):
    bsz, tlen, d = x.shape
    n_prev = 0 if prev is None else prev.shape[0]
    tok = lambda w: pl.BlockSpec((1, rows, w), lambda b, t: (b, t, 0))
    st_spec = lambda n: pl.BlockSpec((n, 1, H_A, DK_A, DV_A), lambda b, t: (0, b, 0, 0, 0))
    in_specs = [tok(d)] + ([st_spec(n_prev)] if n_prev else []) + [_weight_spec(a) for a in weights]
    args = [x] + ([prev] if n_prev else []) + [_weight_arg(a) for a in weights]
    out_specs = [tok(d), st_spec(n_prev + 1)]
    out_shape = [jax.ShapeDtypeStruct((bsz, tlen, d), F32),
                 jax.ShapeDtypeStruct((n_prev + 1, bsz, H_A, DK_A, DV_A), F32)]
    if kv is not None:
        assert min(rows, GLA_SUB) % kv_blk == 0
        tables, kv_weights = kv[:3], kv[3:]
        tab = pl.BlockSpec((rows, LANES), lambda b, t: (t, 0))
        in_specs += [tab] * 3 + [_const_spec(a.shape) for a in kv_weights]
        args += list(tables) + list(kv_weights)
        out_specs += [tok(KV_LORA), tok(ROPE), tok(H_B * NOPE), tok(LANES),
                      pl.BlockSpec((1, rows // kv_blk, VD_B, kv_blk), lambda b, t: (b, t, 0, 0))]
        out_shape += [jax.ShapeDtypeStruct((bsz, tlen, KV_LORA), F32),
                      jax.ShapeDtypeStruct((bsz, tlen, ROPE), F32),
                      jax.ShapeDtypeStruct((bsz, tlen, H_B * NOPE), BF16),
                      jax.ShapeDtypeStruct((bsz, tlen, LANES), BF16),
                      jax.ShapeDtypeStruct((bsz, tlen // kv_blk, VD_B, kv_blk), BF16)]
    return pl.pallas_call(
        functools.partial(_gla_kernel, rows=rows, chunk=chunk, carry=True, n_prev=n_prev,
                          kv_blk=kv_blk if kv is not None else None),
        grid=(bsz, tlen // rows),
        in_specs=in_specs,
        out_specs=tuple(out_specs),
        out_shape=tuple(out_shape),
        scratch_shapes=[pltpu.VMEM((H_A, DK_A, DV_A), F32)] + _gla_scratch(rows, chunk),
        compiler_params=pltpu.CompilerParams(
            dimension_semantics=("parallel", "arbitrary"), vmem_limit_bytes=VMEM_LIMIT),
        name="gla_layer",
    )(*args)


def _gla_sample_layer(x, states, layer, weights, *, groups):
    bsz, t, d = x.shape
    per = bsz // groups
    rows = per * t
    tok = pl.BlockSpec((1, rows, d), lambda g, _: (g, 0, 0))
    st_spec = pl.BlockSpec((per, H_A, DK_A, DV_A), lambda g, _: (g, 0, 0, 0))
    st_in = pl.BlockSpec((None, per, H_A, DK_A, DV_A), lambda g, _: (layer, g, 0, 0, 0))
    y, s_new = pl.pallas_call(
        functools.partial(_gla_kernel, rows=rows, chunk=t, carry=False, n_prev=0, kv_blk=None),
        grid=(groups, 1),
        in_specs=[tok, st_in] + [_weight_spec(a) for a in weights],
        out_specs=(tok, st_spec),
        out_shape=(jax.ShapeDtypeStruct((groups, rows, d), F32),
                   jax.ShapeDtypeStruct((bsz, H_A, DK_A, DV_A), F32)),
        scratch_shapes=_gla_scratch(rows, t),
        compiler_params=pltpu.CompilerParams(
            dimension_semantics=("parallel", "arbitrary"), vmem_limit_bytes=VMEM_LIMIT),
        name="gla_sample_layer",
    )(x.reshape(groups, rows, d), states, *[_weight_arg(a) for a in weights])
    return y.reshape(bsz, t, d), s_new


def _kv_kernel(x_ref, cos_ref, sin_ref, nkv_ref, wdc_ref, wdr_ref, nlat_ref, ckv_ref, kr_ref):
    lat, rr = _kv_down(x_ref[...], nkv_ref, wdc_ref, wdr_ref)
    ckv, kr2 = _kv_finish(lat, rr, cos_ref[...], sin_ref[...], nlat_ref)
    ckv_ref[...] = ckv
    kr_ref[...] = kr2[:, :ROPE]


def _shared_kv_latent(x, cos2, sin2, nkv, wdc, wdr, nlat):
    n, d = x.shape
    args = [x, cos2, sin2, nkv, wdc, wdr, nlat]
    return pl.pallas_call(
        _kv_kernel,
        grid=(1,),
        in_specs=[_const_spec(a.shape) for a in args],
        out_specs=(_const_spec((n, KV_LORA)), _const_spec((n, ROPE))),
        out_shape=(jax.ShapeDtypeStruct((n, KV_LORA), F32), jax.ShapeDtypeStruct((n, ROPE), F32)),
        compiler_params=pltpu.CompilerParams(
            dimension_semantics=("arbitrary",), vmem_limit_bytes=VMEM_LIMIT),
        name="shared_kv",
    )(*args)


def _mla_q(x, cos, sin, npre_ref, win_ref, qn_ref, wq_ref, rope_w):
    hb = _rms(x, npre_ref[...]).astype(BF16)
    proj = _dot(hb, win_ref[...])
    qn = _rms(proj[:, :Q_LORA], qn_ref[...]).astype(BF16)
    qq = _dot(qn, wq_ref[...])
    scale = (NOPE + ROPE) ** -0.5
    n0, n1 = H_B * NOPE, H_B * NOPE + H_B * rope_w
    q_nope = (qq[:, :n0] * scale).astype(BF16)
    q_rope = ((qq[:, n0:n1] * cos + qq[:, n1:] * sin) * scale).astype(BF16)
    return q_nope, q_rope, proj[:, Q_LORA:]


ACC_ROWS = V_B + 16


def _mla_kernel(x_ref, kn_ref, kr_ref, vt_ref, cos_ref, sin_ref, npre_ref, win_ref, qn_ref, wqt_ref,
                wout_ref, npost_ref, y_ref, q_sc, m_sc, acc_sc, alpha_sc, p_sc, *, tq, tk):
    i = pl.program_id(1)
    c = (NOPE + ROPE) ** -0.5 * 1.4426950408889634
    n0, n1 = H_B * NOPE, H_B * (NOPE + ROPE)
    x = x_ref[0]
    hb = _rms(x, npre_ref[...]).astype(BF16)
    qlat = _dot(hb, win_ref[:, :Q_LORA])
    gates = {r: _dot(hb[r:r + tk], win_ref[:, Q_LORA:]) for r in range(0, tq, tk)}
    qnt = _rms(qlat, qn_ref[...]).T.astype(BF16)
    qqt = _dot(wqt_ref[...], qnt)
    q_sc[:n0, :] = (qqt[:n0] * c).astype(BF16)
    cos, sin = cos_ref[...], sin_ref[...]
    for h in range(H_B):
        r0 = n0 + h * ROPE
        x1, x2 = qqt[r0:r0 + HALF], qqt[r0 + HALF:r0 + ROPE]
        q_sc[r0:r0 + HALF, :] = ((x1 * cos - x2 * sin) * c).astype(BF16)
        q_sc[r0 + HALF:r0 + ROPE, :] = ((x2 * cos + x1 * sin) * c).astype(BF16)
    row = lax.broadcasted_iota(jnp.int32, (ROPE, tq), 0)
    q_chunk = jnp.right_shift(i * tq + lax.broadcasted_iota(jnp.int32, (ROPE, tq), 1), CHUNK.bit_length() - 1)
    q_sc[n1:, :] = jnp.where(row > q_chunk, MASKED, 0.0).astype(BF16)
    m_sc[...] = jnp.full(m_sc.shape, -jnp.inf, F32)
    acc_sc[...] = jnp.zeros(acc_sc.shape, F32)

    ones = jnp.ones((ACC_ROWS - V_B, tk), BF16)

    def scores(j, h, lo, hi):
        k0 = pl.multiple_of(j * tk, tk)
        kb = jnp.concatenate(
            [kn_ref[0, pl.ds(k0, tk), h * NOPE:(h + 1) * NOPE], kr_ref[0, pl.ds(k0, tk), :]], axis=1)
        qt = jnp.concatenate(
            [q_sc[h * NOPE:(h + 1) * NOPE, lo:hi], q_sc[n0 + h * ROPE:n0 + (h + 1) * ROPE, lo:hi],
             q_sc[n1:, lo:hi]], axis=0)
        return _dot(kb, qt)

    def softmax(s, h, slot, lo, hi):
        m_prev = m_sc[h, :, lo:hi]
        m_new = jnp.maximum(m_prev, jnp.max(s, axis=0, keepdims=True))
        alpha_sc[slot, h, :, lo:hi] = jnp.exp2(m_prev - m_new)
        p_sc[slot, h, :, lo:hi] = jnp.exp2(s - m_new).astype(BF16)
        m_sc[h, :, lo:hi] = m_new

    def values(j, h, slot, lo, hi):
        vt = jnp.concatenate([vt_ref[0, j, h * V_B:(h + 1) * V_B, :], ones], axis=0)
        acc_sc[h, :, lo:hi] = (alpha_sc[slot, h, :, lo:hi] * acc_sc[h, :, lo:hi]
                               + _dot(vt, p_sc[slot, h, :, lo:hi]))

    for h in range(H_B):
        softmax(scores(0, h, 0, tq), h, 0, 0, tq)

    def block(j, slot_new, slot_old, lo=0):
        s = [scores(j, h, lo, tq) for h in range(H_B)]
        for h in range(H_B):
            values(j - 1, h, slot_old, 0, tq)
        for h in range(H_B):
            softmax(s[h], h, slot_new, lo, tq)

    def pair(jj, carry):
        block(2 * jj + 1, 1, 0)
        block(2 * jj + 2, 0, 1)
        return carry

    def tail_odd(last, lo=0):
        block(last, 1, 0, lo)
        for h in range(H_B):
            values(last, h, 1, lo, tq)

    def tail_even(last):
        for h in range(H_B):
            values(last, h, 0, 0, tq)

    if tq == 2 * tk:
        lax.fori_loop(0, i, pair, 0)
        tail_odd(2 * i + 1, tk)
    else:
        lax.fori_loop(0, i // 2, pair, 0)
        odd = lax.rem(i, 2) == 1
        pl.when(odd)(lambda: tail_odd(i))
        pl.when(jnp.logical_not(odd))(lambda: tail_even(i))

    for r in range(0, tq, tk):
        gate = gates[r]
        outs = []
        for h in range(H_B):
            a = acc_sc[h, :, r:r + tk]
            outs.append(a[:V_B] * (1.0 / a[V_B:V_B + 1]))
        o = jnp.concatenate(outs, axis=0).T
        og = (o * _silu(gate)).astype(BF16)
        y = _dot(og, wout_ref[...])
        y_ref[0, r:r + tk, :] = x[r:r + tk] + _rms(y, npost_ref[...])


def _mla_prompt_layer(x, kn, kr0, vt, cos8t, sin8t, npre, win, qn, wqt, wout, npost, *, tq, tk):
    bsz, tlen, d = x.shape
    assert tq in (tk, 2 * tk) and tk % CHUNK == 0 and vt.shape[3] == tk
    tok = pl.BlockSpec((1, tq, d), lambda b, t: (b, t, 0))
    full = lambda w: pl.BlockSpec((1, tlen, w), lambda b, t: (b, 0, 0))
    tab = pl.BlockSpec((cos8t.shape[0], tq), lambda b, t: (0, t))
    weights = [npre, win, qn, wqt, wout, npost]
    return pl.pallas_call(
        functools.partial(_mla_kernel, tq=tq, tk=tk),
        grid=(bsz, tlen // tq),
        in_specs=[tok, full(kn.shape[2]), full(kr0.shape[2]),
                  pl.BlockSpec((1,) + vt.shape[1:], lambda b, t: (b, 0, 0, 0)), tab, tab]
                 + [_weight_spec(a) for a in weights],
        out_specs=tok,
        out_shape=jax.ShapeDtypeStruct((bsz, tlen, d), F32),
        scratch_shapes=[pltpu.VMEM((H_B * (NOPE + ROPE) + ROPE, tq), BF16),
                        pltpu.VMEM((H_B, 1, tq), F32),
                        pltpu.VMEM((H_B, ACC_ROWS, tq), F32),
                        pltpu.VMEM((2, H_B, 1, tq), F32),
                        pltpu.VMEM((2, H_B, tk, tq), BF16)],
        compiler_params=pltpu.CompilerParams(
            dimension_semantics=("parallel", "arbitrary"), vmem_limit_bytes=VMEM_LIMIT),
        name="mla_prompt_layer",
    )(x, kn, kr0, vt, cos8t, sin8t, *[_weight_arg(a) for a in weights])


def _mla_sample_kernel(x_ref, cckv_ref, ckr_ref, nckv_ref, nkr_ref, cos_ref, sin_ref, npre_ref, win_ref,
                       qn_ref, wq_ref, wuk_ref, wuv_ref, wout_ref, npost_ref, y_ref, *, past):
    x = x_ref[0]
    t = x.shape[0]
    q_nope, q_rope, gate = _mla_q(x, cos_ref[...], sin_ref[...], npre_ref, win_ref, qn_ref, wq_ref, LANES)

    qlat = jnp.concatenate(
        [_dot_nt(q_nope[:, h * NOPE:(h + 1) * NOPE], wuk_ref[:, h * NOPE:(h + 1) * NOPE]) for h in range(H_B)],
        axis=0).astype(BF16)
    qr = jnp.concatenate([q_rope[:, h * LANES:h * LANES + ROPE] for h in range(H_B)], axis=0)

    c_ckv = cckv_ref[0].astype(BF16)
    n_ckv = nckv_ref[0].astype(BF16)
    s_c = _dot_nt(qlat, c_ckv) + _dot_nt(qr, ckr_ref[0].astype(BF16))
    s_n = _dot_nt(qlat, n_ckv) + _dot_nt(qr, nkr_ref[0].astype(BF16))

    cshift = CHUNK.bit_length() - 1

    def chunk_mask(s, k_start):
        rows = lax.broadcasted_iota(jnp.int32, s.shape, 0)
        cols = lax.broadcasted_iota(jnp.int32, s.shape, 1)
        q_chunk = jnp.right_shift(past + jnp.bitwise_and(rows, t - 1), cshift)
        k_chunk = jnp.right_shift(k_start + cols, cshift)
        return jnp.where(k_chunk <= q_chunk, s, -jnp.inf)

    s_c = chunk_mask(s_c, 0)
    s_n = chunk_mask(s_n, past)
    m = jnp.maximum(jnp.max(s_c, axis=-1, keepdims=True), jnp.max(s_n, axis=-1, keepdims=True))
    p_c = jnp.exp(s_c - m)
    p_n = jnp.exp(s_n - m)
    l = jnp.sum(p_c, axis=-1, keepdims=True) + jnp.sum(p_n, axis=-1, keepdims=True)
    olat = (_dot(p_c.astype(BF16), c_ckv) + _dot(p_n.astype(BF16), n_ckv)) * (1.0 / l)
    olat = olat.astype(BF16)
    o = jnp.concatenate(
        [_dot(olat[h * t:(h + 1) * t], wuv_ref[:, h * V_B:(h + 1) * V_B]) for h in range(H_B)], axis=1)
    og = (o * _silu(gate)).astype(BF16)
    y = _dot(og, wout_ref[...])
    y_ref[0] = x + _rms(y, npost_ref[...])


def _mla_sample_layer(x, cache_ckv, cache_kr, new_ckv, new_kr, cos, sin, npre, win, qn, wq, wuk, wuv,
                      wout, npost):
    bsz, t, d = x.shape
    past = cache_ckv.shape[1]
    assert t & (t - 1) == 0
    per_b = lambda a: pl.BlockSpec((1,) + a.shape[1:], lambda b: (b, 0, 0))
    consts = [cos, sin, npre, win, qn, wq, wuk, wuv, wout, npost]
    return pl.pallas_call(
        functools.partial(_mla_sample_kernel, past=past),
        grid=(bsz,),
        in_specs=[per_b(x), per_b(cache_ckv), per_b(cache_kr), per_b(new_ckv), per_b(new_kr)]
                 + [_weight_spec(a) for a in consts],
        out_specs=per_b(x),
        out_shape=jax.ShapeDtypeStruct((bsz, t, d), F32),
        compiler_params=pltpu.CompilerParams(
            dimension_semantics=("parallel",), vmem_limit_bytes=VMEM_LIMIT),
        name="mla_sample_layer",
    )(x, cache_ckv, cache_kr, new_ckv, new_kr, *[_weight_arg(a) for a in consts])


def _rope_tables(pos_offset, tlen):
    inv = jnp.power(ROPE_BASE, -jnp.arange(HALF, dtype=F32) / HALF)
    ang = (pos_offset + jnp.arange(tlen)).astype(F32)[:, None] * inv[None, :]
    cos = jnp.cos(ang)
    sin = jnp.sin(ang)
    return jnp.concatenate([cos, cos], axis=1), jnp.concatenate([sin, sin], axis=1)


def _rot_cols(w):
    return jnp.concatenate([-w[..., HALF:], w[..., :HALF]], axis=-1)


def _row(a):
    return a.reshape(1, -1)


def kernel(x_prompt, x_sample, state_gla, cache_ckv, cache_krope, a_norm_pre, a_w_in, a_w_gate2, a_b_gate,
           a_head_norm, a_w_out, a_norm_post, kv_norm, kv_w_down, kv_latent_norm, kv_w_uk, kv_w_uv,
           b_norm_pre, b_w_in, b_q_norm, b_w_uq, b_w_out, b_norm_post):
    past = cache_ckv.shape[1]
    t_p, t_s = x_prompt.shape[1], x_sample.shape[1]

    a_win = a_w_in.astype(BF16)
    a_wglr = jnp.pad(a_w_in[:, :, 2 * KD_A + 2 * VD_A:],
                     ((0, 0), (0, 0), (0, 2 * LANES - GATE_RANK))).astype(BF16)
    a_wg2 = a_w_gate2.astype(BF16)
    a_wout = a_w_out.astype(BF16)
    wdc = kv_w_down[:, :KV_LORA].astype(BF16)
    wdr_cols = kv_w_down[:, KV_LORA:]
    wdr = jnp.concatenate([wdr_cols, wdr_cols, _rot_cols(wdr_cols), _rot_cols(wdr_cols)], axis=1).astype(BF16)
    wuk = kv_w_uk.astype(BF16)
    wuv = kv_w_uv.astype(BF16)
    b_win = b_w_in.astype(BF16)
    b_wout = b_w_out.astype(BF16)
    uq = b_w_uq.reshape(N_B, Q_LORA, H_B, NOPE + ROPE)
    uq_nope = uq[..., :NOPE].reshape(N_B, Q_LORA, H_B * NOPE)
    uq_rope = uq[..., NOPE:]
    uq_rot = _rot_cols(uq_rope)
    wqt_p = jnp.concatenate([uq_nope, uq_rope.reshape(N_B, Q_LORA, H_B * ROPE)],
                            axis=-1).astype(BF16).transpose(0, 2, 1)
    wuvt = wuv.T
    pad = lambda w: jnp.pad(w, ((0, 0), (0, 0), (0, 0), (0, LANES - ROPE))).reshape(N_B, Q_LORA, H_B * LANES)
    wq_s = jnp.concatenate([uq_nope, pad(uq_rope), pad(uq_rot)], axis=-1).astype(BF16)

    cos_p, sin_p = _rope_tables(0, t_p)
    cos_s, sin_s = _rope_tables(past, t_s)
    tile2 = lambda a: jnp.concatenate([a, a], axis=1)
    cos8t_p, sin8t_p = cos_p[:, :HALF].T, sin_p[:, :HALF].T
    padl = lambda a: jnp.tile(jnp.pad(a, ((0, 0), (0, LANES - ROPE))), (1, H_B))
    cos8_s, sin8_s = padl(cos_s), padl(sin_s)

    def lay(a, l):
        return _Layer(a.reshape(a.shape[0], 1, -1) if a.ndim == 2 else a, l)

    def gla_weights(l):
        return (lay(a_norm_pre, l), lay(a_win, l), lay(a_wglr, l), lay(a_wg2, l), lay(a_b_gate, l),
                lay(a_head_norm, l), lay(a_wout, l), lay(a_norm_post, l))

    assert t_p // CHUNK <= LANES - ROPE
    koh = jax.nn.one_hot(ROPE + jnp.arange(t_p) // CHUNK, LANES, dtype=F32)
    kv_args = (tile2(cos_p), tile2(sin_p), koh, _row(kv_norm), wdc, wdr, _row(kv_latent_norm), wuk, wuvt)
    x, st_p = x_prompt, None
    for l in range(N_A - 1):
        x, st_p = _gla_prompt_layer(x, st_p, gla_weights(l), rows=GLA_ROWS, chunk=CHUNK)
    x, st_p, ckv_p, kr_p, kn, kr0, vt = _gla_prompt_layer(
        x, st_p, gla_weights(N_A - 1), kv_args, rows=GLA_ROWS, chunk=CHUNK, kv_blk=ATT_K)
    for j in range(N_B):
        x = _mla_prompt_layer(x, kn, kr0, vt, cos8t_p, sin8t_p, lay(b_norm_pre, j), lay(b_win, j),
                              lay(b_q_norm, j), lay(wqt_p, j), lay(b_wout, j), lay(b_norm_post, j),
                              tq=ATT_Q, tk=ATT_K)
    y_prompt = x

    assert t_s <= CHUNK
    bsz_s = x_sample.shape[0]
    x, states = x_sample, []
    for l in range(N_A):
        x, s_new = _gla_sample_layer(x, state_gla, l, gla_weights(l), groups=SAMPLE_GROUPS)
        states.append(s_new)
    st_s = jnp.stack(states)
    per_row = lambda a: jnp.tile(tile2(a), (bsz_s, 1))
    ckv_s, kr_s = _shared_kv_latent(x.reshape(bsz_s * t_s, D_MODEL), per_row(cos_s), per_row(sin_s),
                                    _row(kv_norm), wdc, wdr, _row(kv_latent_norm))
    ckv_s = ckv_s.reshape(bsz_s, t_s, KV_LORA)
    kr_s = kr_s.reshape(bsz_s, t_s, ROPE)
    for j in range(N_B):
        x = _mla_sample_layer(x, cache_ckv, cache_krope, ckv_s, kr_s, cos8_s, sin8_s,
                              lay(b_norm_pre, j), lay(b_win, j), lay(b_q_norm, j), lay(wq_s, j), wuk, wuv,
                              lay(b_wout, j), lay(b_norm_post, j))
    y_sample = x

    return (y_prompt, y_sample, st_p, st_s, ckv_p, kr_p, ckv_s, kr_s)
```

```python
import functools

import jax
import jax.numpy as jnp
from jax import lax
from jax.experimental import pallas as pl
from jax.experimental.pallas import tpu as pltpu

F32 = jnp.float32
BF16 = jnp.bfloat16

D_MODEL = 1024
CHUNK = 64
N_A = 2
N_B = 2
H_A = 4
DK_A = 128
DV_A = 256
KD_A = H_A * DK_A
VD_A = H_A * DV_A
GATE_RANK = 16
GATE_TAU = 16.0
H_B = 8
NOPE = 128
ROPE = 64
HALF = ROPE // 2
V_B = 128
VD_B = H_B * V_B
Q_LORA = 384
KV_LORA = 256
ROPE_BASE = 10000.0
EPS = 1e-6

LANES = 128
VMEM_LIMIT = 56 * 1024 * 1024

GLA_ROWS = 512
GLA_SUB = 256
SAMPLE_GROUPS = 2
ATT_Q = 512
ATT_K = 256
MASKED = -1e30


def _dot(a, b):
    return jnp.dot(a, b, preferred_element_type=F32)


def _dot_nt(a, b):
    return lax.dot_general(a, b, (((1,), (1,)), ((), ())), preferred_element_type=F32)


def _dot_tn(a, b):
    return lax.dot_general(a, b, (((0,), (0,)), ((), ())), preferred_element_type=F32)


def _rms(xf, g):
    ms = jnp.mean(xf * xf, axis=-1, keepdims=True)
    return xf * lax.rsqrt(ms + EPS) * g


def _silu(g):
    return g * (1.0 / (1.0 + jnp.exp(-g)))


def _const_spec(shape):
    return pl.BlockSpec(shape, lambda *_: (0,) * len(shape))


class _Layer:
    def __init__(self, stacked, l):
        self.stacked, self.l, self.shape = stacked, l, stacked.shape[1:]


def _weight_spec(w):
    if isinstance(w, _Layer):
        l, nd = w.l, len(w.shape)
        return pl.BlockSpec((None,) + w.shape, lambda *_: (l,) + (0,) * nd)
    return _const_spec(w.shape)


def _weight_arg(w):
    return w.stacked if isinstance(w, _Layer) else w


def _kv_down(xb, nkv_ref, wdc_ref, wdr_ref):
    hb = _rms(xb, nkv_ref[...]).astype(BF16)
    return _dot(hb, wdc_ref[...]), _dot(hb, wdr_ref[...])


def _kv_finish(lat, rr, cos, sin, nlat_ref):
    return _rms(lat, nlat_ref[...]), rr[:, :LANES] * cos + rr[:, LANES:] * sin


def _gla_kernel(*refs, rows, chunk, carry, n_prev, kv_blk):
    refs = list(refs)
    x_ref = refs.pop(0)
    s0_ref = None if carry else refs.pop(0)
    prev_ref = refs.pop(0) if n_prev else None
    npre_ref, win_ref, wglr_ref, wg2_ref, bg_ref, hn_ref, wout_ref, npost_ref = refs[:8]
    del refs[:8]
    if kv_blk:
        (cos_ref, sin_ref, koh_ref, nkv_ref, wdc_ref, wdr_ref, nlat_ref, wuk_ref, wuvt_ref) = refs[:9]
        del refs[:9]
    y_ref, sout_ref = refs[:2]
    del refs[:2]
    if kv_blk:
        ckv_ref, kr_ref, kn_ref, kr0_ref, vt_ref = refs[:5]
        del refs[:5]
    st_sc = refs.pop(0) if carry else None
    o_sc, qe_sc, u_sc, stb_sc = refs
    t = pl.program_id(1)

    if carry:
        @pl.when(t == 0)
        def _():
            st_sc[...] = jnp.zeros(st_sc.shape, F32)

    shift = chunk.bit_length() - 1
    sub = min(rows, GLA_SUB)
    blocks = range(0, rows, sub)
    xs, z, proj, b = {}, {}, {}, {}
    n_main = 2 * KD_A + 2 * VD_A
    for r in blocks:
        xs[r] = x_ref[0, r:r + sub, :]
        hb = _rms(xs[r], npre_ref[...]).astype(BF16)
        glr = _dot(hb, wglr_ref[...])[:, :GATE_RANK].astype(BF16)
        proj_qk = _dot(hb, win_ref[:, :2 * KD_A])
        z[r] = _dot(glr, wg2_ref[...]) + bg_ref[...]
        proj[r] = jnp.concatenate([proj_qk, _dot(hb, win_ref[:, 2 * KD_A:n_main])], axis=1)

    ri = lax.broadcasted_iota(jnp.int32, (sub, sub), 0)
    ci = lax.broadcasted_iota(jnp.int32, (sub, sub), 1)
    inchunk = jnp.where(jnp.right_shift(ri, shift) == jnp.right_shift(ci, shift), ci, sub) <= ri
    tri = jnp.where(inchunk, 1.0, 0.0).astype(BF16)
    for r in blocks:
        g = (jnp.minimum(z[r], 0.0) - jnp.log(1.0 + jnp.exp(-jnp.abs(z[r])))) * (1.0 / GATE_TAU)
        g_hi = g.astype(BF16)
        g_lo = (g - g_hi.astype(F32)).astype(BF16)
        b[r] = _dot(tri, g_hi) + _dot(tri, g_lo)

    scale = DK_A ** -0.5
    nc = rows // chunk
    decay = [[None] * nc for _ in range(H_A)]

    for h in range(H_A):
        for r in blocks:
            chunks = range(r // chunk, (r + sub) // chunk)
            bh = b[r][:, h * DK_A:(h + 1) * DK_A]
            ends = {c: bh[(c + 1) * chunk - 1 - r:(c + 1) * chunk - r, :] for c in chunks}
            bl = jnp.concatenate([jnp.broadcast_to(ends[c], (chunk, DK_A)) for c in chunks], axis=0)
            qh = proj[r][:, h * DK_A:(h + 1) * DK_A]
            kh = proj[r][:, KD_A + h * DK_A:KD_A + (h + 1) * DK_A]
            vh = proj[r][:, 2 * KD_A + h * DV_A:2 * KD_A + (h + 1) * DV_A].astype(BF16)
            qe = (qh * scale * jnp.exp(bh)).astype(BF16)
            ke = (kh * jnp.exp(-bh)).astype(BF16)
            kd = (kh * jnp.exp(bl - bh)).astype(BF16)
            s_raw = _dot_nt(qe, ke)
            qe_sc[r:r + sub, h * DK_A:(h + 1) * DK_A] = qe
            for c in chunks:
                r0, r1 = c * chunk - r, (c + 1) * chunk - r
                u_sc[h, c] = _dot_tn(kd[r0:r1], vh[r0:r1])
                dcol = jnp.broadcast_to(jnp.exp(ends[c]), (LANES, DK_A)).T
                decay[h][c] = jnp.concatenate([dcol] * (DV_A // LANES), axis=1)
            s = jnp.where(inchunk, s_raw, 0.0).astype(BF16)
            o_sc[r:r + sub, h * DV_A:(h + 1) * DV_A] = _dot(s, vh)

    for h in range(H_A):
        if carry:
            st = st_sc[h]
        for c in range(nc):
            if not carry:
                st = s0_ref[c, h]
            stb_sc[h, c] = st.astype(BF16)
            st_new = st * decay[h][c] + u_sc[h, c]
            if carry:
                st = st_new
            else:
                sout_ref[c, h] = st_new
        if carry:
            st_sc[h] = st
        for c in range(nc):
            r0, r1 = c * chunk, (c + 1) * chunk
            o_sc[r0:r1, h * DV_A:(h + 1) * DV_A] += _dot(qe_sc[r0:r1, h * DK_A:(h + 1) * DK_A], stb_sc[h, c])

    x_new = {}
    for r in blocks:
        gate = proj[r][:, 2 * KD_A + VD_A:2 * KD_A + 2 * VD_A]
        heads = []
        for h in range(H_A):
            heads.append(_rms(o_sc[r:r + sub, h * DV_A:(h + 1) * DV_A], hn_ref[:, h * DV_A:(h + 1) * DV_A]))
        og = (jnp.concatenate(heads, axis=1) * _silu(gate)).astype(BF16)
        y = _dot(og, wout_ref[...])
        x_new[r] = xs[r] + _rms(y, npost_ref[...])
        y_ref[0, r:r + sub, :] = x_new[r]

    if kv_blk:
        down = {r: _kv_down(x_new[r], nkv_ref, wdc_ref, wdr_ref) for r in blocks}
        for r in blocks:
            ckv, kr2 = _kv_finish(*down[r], cos_ref[r:r + sub, :], sin_ref[r:r + sub, :], nlat_ref)
            ckv_ref[0, r:r + sub, :] = ckv
            kr_ref[0, r:r + sub, :] = kr2[:, :ROPE]
            cb = ckv.astype(BF16)
            kn_ref[0, r:r + sub, :] = _dot(cb, wuk_ref[...]).astype(BF16)
            lane = lax.broadcasted_iota(jnp.int32, kr2.shape, 1)
            kr0_ref[0, r:r + sub, :] = jnp.where(lane < ROPE, kr2, koh_ref[r:r + sub, :]).astype(BF16)
            vt = _dot_nt(wuvt_ref[...], cb).astype(BF16)
            for i in range(sub // kv_blk):
                vt_ref[0, r // kv_blk + i] = vt[:, i * kv_blk:(i + 1) * kv_blk]

    if carry:
        @pl.when(t == pl.num_programs(1) - 1)
        def _():
            for li in range(n_prev):
                sout_ref[li, 0] = prev_ref[li, 0]
            for h in range(H_A):
                sout_ref[n_prev, 0, h] = st_sc[h]


def _gla_scratch(rows, chunk):
    return [pltpu.VMEM((rows, VD_A), F32), pltpu.VMEM((rows, KD_A), BF16),
            pltpu.VMEM((H_A, rows // chunk, DK_A, DV_A), F32),
            pltpu.VMEM((H_A, rows // chunk, DK_A, DV_A), BF16)]


def _gla_prompt_layer(x, prev, weights, kv=None, *, rows, chunk, kv_blk=None):
    bsz, tlen, d = x.shape
    n_prev = 0 if prev is None else prev.shape[0]
    tok = lambda w: pl.BlockSpec((1, rows, w), lambda b, t: (b, t, 0))
    st_spec = lambda n: pl.BlockSpec((n, 1, H_A, DK_A, DV_A), lambda b, t: (0, b, 0, 0, 0))
    in_specs = [tok(d)] + ([st_spec(n_prev)] if n_prev else []) + [_weight_spec(a) for a in weights]
    args = [x] + ([prev] if n_prev else []) + [_weight_arg(a) for a in weights]
    out_specs = [tok(d), st_spec(n_prev + 1)]
    out_shape = [jax.ShapeDtypeStruct((bsz, tlen, d), F32),
                 jax.ShapeDtypeStruct((n_prev + 1, bsz, H_A, DK_A, DV_A), F32)]
    if kv is not None:
        assert min(rows, GLA_SUB) % kv_blk == 0
        tables, kv_weights = kv[:3], kv[3:]
        tab = pl.BlockSpec((rows, LANES), lambda b, t: (t, 0))
        in_specs += [tab] * 3 + [_const_spec(a.shape) for a in kv_weights]
        args += list(tables) + list(kv_weights)
        out_specs += [tok(KV_LORA), tok(ROPE), tok(H_B * NOPE), tok(LANES),
                      pl.BlockSpec((1, rows // kv_blk, VD_B, kv_blk), lambda b, t: (b, t, 0, 0))]
        out_shape += [jax.ShapeDtypeStruct((bsz, tlen, KV_LORA), F32),
                      jax.ShapeDtypeStruct((bsz, tlen, ROPE), F32),
                      jax.ShapeDtypeStruct((bsz, tlen, H_B * NOPE), BF16),
                      jax.ShapeDtypeStruct((bsz, tlen, LANES), BF16),
                      jax.ShapeDtypeStruct((bsz, tlen // kv_blk, VD_B, kv_blk), BF16)]
    return pl.pallas_call(
        functools.partial(_gla_kernel, rows=rows, chunk=chunk, carry=True, n_prev=n_prev,
                          kv_blk=kv_blk if kv is not None else None),
        grid=(bsz, tlen // rows),
        in_specs=in_specs,
        out_specs=tuple(out_specs),
        out_shape=tuple(out_shape),
        scratch_shapes=[pltpu.VMEM((H_A, DK_A, DV_A), F32)] + _gla_scratch(rows, chunk),
        compiler_params=pltpu.CompilerParams(
            dimension_semantics=("parallel", "arbitrary"), vmem_limit_bytes=VMEM_LIMIT),
        name="gla_layer",
    )(*args)


def _gla_sample_layer(x, states, layer, weights, *, groups):
    bsz, t, d = x.shape
    per = bsz // groups
    rows = per * t
    tok = pl.BlockSpec((1, rows, d), lambda g, _: (g, 0, 0))
    st_spec = pl.BlockSpec((per, H_A, DK_A, DV_A), lambda g, _: (g, 0, 0, 0))
    st_in = pl.BlockSpec((None, per, H_A, DK_A, DV_A), lambda g, _: (layer, g, 0, 0, 0))
    y, s_new = pl.pallas_call(
        functools.partial(_gla_kernel, rows=rows, chunk=t, carry=False, n_prev=0, kv_blk=None),
        grid=(groups, 1),
        in_specs=[tok, st_in] + [_weight_spec(a) for a in weights],
        out_specs=(tok, st_spec),
        out_shape=(jax.ShapeDtypeStruct((groups, rows, d), F32),
                   jax.ShapeDtypeStruct((bsz, H_A, DK_A, DV_A), F32)),
        scratch_shapes=_gla_scratch(rows, t),
        compiler_params=pltpu.CompilerParams(
            dimension_semantics=("parallel", "arbitrary"), vmem_limit_bytes=VMEM_LIMIT),
        name="gla_sample_layer",
    )(x.reshape(groups, rows, d), states, *[_weight_arg(a) for a in weights])
    return y.reshape(bsz, t, d), s_new


def _kv_kernel(x_ref, cos_ref, sin_ref, nkv_ref, wdc_ref, wdr_ref, nlat_ref, ckv_ref, kr_ref):
    lat, rr = _kv_down(x_ref[...], nkv_ref, wdc_ref, wdr_ref)
    ckv, kr2 = _kv_finish(lat, rr, cos_ref[...], sin_ref[...], nlat_ref)
    ckv_ref[...] = ckv
    kr_ref[...] = kr2[:, :ROPE]


def _shared_kv_latent(x, cos2, sin2, nkv, wdc, wdr, nlat):
    n, d = x.shape
    args = [x, cos2, sin2, nkv, wdc, wdr, nlat]
    return pl.pallas_call(
        _kv_kernel,
        grid=(1,),
        in_specs=[_const_spec(a.shape) for a in args],
        out_specs=(_const_spec((n, KV_LORA)), _const_spec((n, ROPE))),
        out_shape=(jax.ShapeDtypeStruct((n, KV_LORA), F32), jax.ShapeDtypeStruct((n, ROPE), F32)),
        compiler_params=pltpu.CompilerParams(
            dimension_semantics=("arbitrary",), vmem_limit_bytes=VMEM_LIMIT),
        name="shared_kv",
    )(*args)


def _mla_q(x, cos, sin, npre_ref, win_ref, qn_ref, wq_ref, rope_w):
    hb = _rms(x, npre_ref[...]).astype(BF16)
    proj = _dot(hb, win_ref[...])
    qn = _rms(proj[:, :Q_LORA], qn_ref[...]).astype(BF16)
    qq = _dot(qn, wq_ref[...])
    scale = (NOPE + ROPE) ** -0.5
    n0, n1 = H_B * NOPE, H_B * NOPE + H_B * rope_w
    q_nope = (qq[:, :n0] * scale).astype(BF16)
    q_rope = ((qq[:, n0:n1] * cos + qq[:, n1:] * sin) * scale).astype(BF16)
    return q_nope, q_rope, proj[:, Q_LORA:]


ACC_ROWS = V_B + 16


def _mla_kernel(x_ref, kn_ref, kr_ref, vt_ref, cos_ref, sin_ref, npre_ref, win_ref, qn_ref, wqt_ref,
                wout_ref, npost_ref, y_ref, q_sc, m_sc, acc_sc, alpha_sc, p_sc, *, tq, tk):
    i = pl.program_id(1)
    c = (NOPE + ROPE) ** -0.5 * 1.4426950408889634
    n0, n1 = H_B * NOPE, H_B * (NOPE + ROPE)
    x = x_ref[0]
    hb = _rms(x, npre_ref[...]).astype(BF16)
    qlat = _dot(hb, win_ref[:, :Q_LORA])
    gates = {r: _dot(hb[r:r + tk], win_ref[:, Q_LORA:]) for r in range(0, tq, tk)}
    qnt = _rms(qlat, qn_ref[...]).T.astype(BF16)
    qqt = _dot(wqt_ref[...], qnt)
    q_sc[:n0, :] = (qqt[:n0] * c).astype(BF16)
    cos, sin = cos_ref[...], sin_ref[...]
    for h in range(H_B):
        r0 = n0 + h * ROPE
        x1, x2 = qqt[r0:r0 + HALF], qqt[r0 + HALF:r0 + ROPE]
        q_sc[r0:r0 + HALF, :] = ((x1 * cos - x2 * sin) * c).astype(BF16)
        q_sc[r0 + HALF:r0 + ROPE, :] = ((x2 * cos + x1 * sin) * c).astype(BF16)
    row = lax.broadcasted_iota(jnp.int32, (ROPE, tq), 0)
    q_chunk = jnp.right_shift(i * tq + lax.broadcasted_iota(jnp.int32, (ROPE, tq), 1), CHUNK.bit_length() - 1)
    q_sc[n1:, :] = jnp.where(row > q_chunk, MASKED, 0.0).astype(BF16)
    m_sc[...] = jnp.full(m_sc.shape, -jnp.inf, F32)
    acc_sc[...] = jnp.zeros(acc_sc.shape, F32)

    ones = jnp.ones((ACC_ROWS - V_B, tk), BF16)

    def scores(j, h, lo, hi):
        k0 = pl.multiple_of(j * tk, tk)
        kb = jnp.concatenate(
            [kn_ref[0, pl.ds(k0, tk), h * NOPE:(h + 1) * NOPE], kr_ref[0, pl.ds(k0, tk), :]], axis=1)
        qt = jnp.concatenate(
            [q_sc[h * NOPE:(h + 1) * NOPE, lo:hi], q_sc[n0 + h * ROPE:n0 + (h + 1) * ROPE, lo:hi],
             q_sc[n1:, lo:hi]], axis=0)
        return _dot(kb, qt)

    def softmax(s, h, slot, lo, hi):
        m_prev = m_sc[h, :, lo:hi]
        m_new = jnp.maximum(m_prev, jnp.max(s, axis=0, keepdims=True))
        alpha_sc[slot, h, :, lo:hi] = jnp.exp2(m_prev - m_new)
        p_sc[slot, h, :, lo:hi] = jnp.exp2(s - m_new).astype(BF16)
        m_sc[h, :, lo:hi] = m_new

    def values(j, h, slot, lo, hi):
        vt = jnp.concatenate([vt_ref[0, j, h * V_B:(h + 1) * V_B, :], ones], axis=0)
        acc_sc[h, :, lo:hi] = (alpha_sc[slot, h, :, lo:hi] * acc_sc[h, :, lo:hi]
                               + _dot(vt, p_sc[slot, h, :, lo:hi]))

    for h in range(H_B):
        softmax(scores(0, h, 0, tq), h, 0, 0, tq)

    def block(j, slot_new, slot_old, lo=0):
        s = [scores(j, h, lo, tq) for h in range(H_B)]
        for h in range(H_B):
            values(j - 1, h, slot_old, 0, tq)
        for h in range(H_B):
            softmax(s[h], h, slot_new, lo, tq)

    def pair(jj, carry):
        block(2 * jj + 1, 1, 0)
        block(2 * jj + 2, 0, 1)
        return carry

    def tail_odd(last, lo=0):
        block(last, 1, 0, lo)
        for h in range(H_B):
            values(last, h, 1, lo, tq)

    def tail_even(last):
        for h in range(H_B):
            values(last, h, 0, 0, tq)

    if tq == 2 * tk:
        lax.fori_loop(0, i, pair, 0)
        tail_odd(2 * i + 1, tk)
    else:
        lax.fori_loop(0, i // 2, pair, 0)
        odd = lax.rem(i, 2) == 1
        pl.when(odd)(lambda: tail_odd(i))
        pl.when(jnp.logical_not(odd))(lambda: tail_even(i))

    for r in range(0, tq, tk):
        gate = gates[r]
        outs = []
        for h in range(H_B):
            a = acc_sc[h, :, r:r + tk]
            outs.append(a[:V_B] * (1.0 / a[V_B:V_B + 1]))
        o = jnp.concatenate(outs, axis=0).T
        og = (o * _silu(gate)).astype(BF16)
        y = _dot(og, wout_ref[...])
        y_ref[0, r:r + tk, :] = x[r:r + tk] + _rms(y, npost_ref[...])


def _mla_prompt_layer(x, kn, kr0, vt, cos8t, sin8t, npre, win, qn, wqt, wout, npost, *, tq, tk):
    bsz, tlen, d = x.shape
    assert tq in (tk, 2 * tk) and tk % CHUNK == 0 and vt.shape[3] == tk
    tok = pl.BlockSpec((1, tq, d), lambda b, t: (b, t, 0))
    full = lambda w: pl.BlockSpec((1, tlen, w), lambda b, t: (b, 0, 0))
    tab = pl.BlockSpec((cos8t.shape[0], tq), lambda b, t: (0, t))
    weights = [npre, win, qn, wqt, wout, npost]
    return pl.pallas_call(
        functools.partial(_mla_kernel, tq=tq, tk=tk),
        grid=(bsz, tlen // tq),
        in_specs=[tok, full(kn.shape[2]), full(kr0.shape[2]),
                  pl.BlockSpec((1,) + vt.shape[1:], lambda b, t: (b, 0, 0, 0)), tab, tab]
                 + [_weight_spec(a) for a in weights],
        out_specs=tok,
        out_shape=jax.ShapeDtypeStruct((bsz, tlen, d), F32),
        scratch_shapes=[pltpu.VMEM((H_B * (NOPE + ROPE) + ROPE, tq), BF16),
                        pltpu.VMEM((H_B, 1, tq), F32),
                        pltpu.VMEM((H_B, ACC_ROWS, tq), F32),
                        pltpu.VMEM((2, H_B, 1, tq), F32),
                        pltpu.VMEM((2, H_B, tk, tq), BF16)],
        compiler_params=pltpu.CompilerParams(
            dimension_semantics=("parallel", "arbitrary"), vmem_limit_bytes=VMEM_LIMIT),
        name="mla_prompt_layer",
    )(x, kn, kr0, vt, cos8t, sin8t, *[_weight_arg(a) for a in weights])


def _mla_sample_q_kernel(x_ref, cos_ref, sin_ref, npre_ref, win_ref, qn_ref, wq_ref, wuk_ref,
                         qlat_ref, qr_ref, gate_ref):
    q_nope, q_rope, gate = _mla_q(x_ref[...], cos_ref[...], sin_ref[...], npre_ref, win_ref, qn_ref,
                                  wq_ref, LANES)
    for h in range(H_B):
        qlat_ref[:, h * KV_LORA:(h + 1) * KV_LORA] = _dot_nt(
            q_nope[:, h * NOPE:(h + 1) * NOPE], wuk_ref[:, h * NOPE:(h + 1) * NOPE]).astype(BF16)
    qr_ref[...] = q_rope
    gate_ref[...] = gate


def _mla_sample_out_kernel(x_ref, olat_ref, gate_ref, wuv_ref, wout_ref, npost_ref, y_ref):
    o = jnp.concatenate(
        [_dot(olat_ref[:, h * KV_LORA:(h + 1) * KV_LORA], wuv_ref[:, h * V_B:(h + 1) * V_B])
         for h in range(H_B)], axis=1)
    og = (o * _silu(gate_ref[...])).astype(BF16)
    y = _dot(og, wout_ref[...])
    y_ref[...] = x_ref[...] + _rms(y, npost_ref[...])


def _mla_sample_attn_kernel(qlat_ref, qr_ref, cckv_ref, ckr_ref, nckv_ref, nkr_ref, olat_ref, *, past):
    t = qlat_ref.shape[1]
    qlat = jnp.concatenate([qlat_ref[0, :, h * KV_LORA:(h + 1) * KV_LORA] for h in range(H_B)], axis=0)
    qr = jnp.concatenate([qr_ref[0, :, h * LANES:h * LANES + ROPE] for h in range(H_B)], axis=0)

    c_ckv = cckv_ref[0].astype(BF16)
    n_ckv = nckv_ref[0].astype(BF16)
    s_c = _dot_nt(qlat, c_ckv) + _dot_nt(qr, ckr_ref[0].astype(BF16))
    s_n = _dot_nt(qlat, n_ckv) + _dot_nt(qr, nkr_ref[0].astype(BF16))

    cshift = CHUNK.bit_length() - 1

    def chunk_mask(s, k_start):
        rows = lax.broadcasted_iota(jnp.int32, s.shape, 0)
        cols = lax.broadcasted_iota(jnp.int32, s.shape, 1)
        q_chunk = jnp.right_shift(past + jnp.bitwise_and(rows, t - 1), cshift)
        k_chunk = jnp.right_shift(k_start + cols, cshift)
        return jnp.where(k_chunk <= q_chunk, s, -jnp.inf)

    s_c = chunk_mask(s_c, 0)
    s_n = chunk_mask(s_n, past)
    m = jnp.maximum(jnp.max(s_c, axis=-1, keepdims=True), jnp.max(s_n, axis=-1, keepdims=True))
    p_c = jnp.exp(s_c - m)
    p_n = jnp.exp(s_n - m)
    l = jnp.sum(p_c, axis=-1, keepdims=True) + jnp.sum(p_n, axis=-1, keepdims=True)
    olat = ((_dot(p_c.astype(BF16), c_ckv) + _dot(p_n.astype(BF16), n_ckv)) * (1.0 / l)).astype(BF16)
    for h in range(H_B):
        olat_ref[0, :, h * KV_LORA:(h + 1) * KV_LORA] = olat[h * t:(h + 1) * t]


def _one_step_call(body, args, out_shapes, name):
    return pl.pallas_call(
        body,
        grid=(1,),
        in_specs=[_weight_spec(a) for a in args],
        out_specs=tuple(_const_spec(s.shape) for s in out_shapes),
        out_shape=tuple(out_shapes),
        compiler_params=pltpu.CompilerParams(
            dimension_semantics=("arbitrary",), vmem_limit_bytes=VMEM_LIMIT),
        name=name,
    )(*[_weight_arg(a) for a in args])


def _mla_sample_layer(x, cache_ckv, cache_kr, new_ckv, new_kr, cos, sin, npre, win, qn, wq, wuk, wuv,
                      wout, npost):
    bsz, t, d = x.shape
    past = cache_ckv.shape[1]
    assert t & (t - 1) == 0
    n = bsz * t
    x2 = x.reshape(n, d)
    qlat, qr, gate = _one_step_call(
        _mla_sample_q_kernel, [x2, cos, sin, npre, win, qn, wq, wuk],
        [jax.ShapeDtypeStruct((n, H_B * KV_LORA), BF16), jax.ShapeDtypeStruct((n, H_B * LANES), BF16),
         jax.ShapeDtypeStruct((n, VD_B), F32)], "mla_sample_q")
    per_b = lambda a: pl.BlockSpec((1,) + a.shape[1:], lambda b: (b, 0, 0))
    attn_args = [qlat.reshape(bsz, t, -1), qr.reshape(bsz, t, -1), cache_ckv, cache_kr, new_ckv, new_kr]
    olat = pl.pallas_call(
        functools.partial(_mla_sample_attn_kernel, past=past),
        grid=(bsz,),
        in_specs=[per_b(a) for a in attn_args],
        out_specs=pl.BlockSpec((1, t, H_B * KV_LORA), lambda b: (b, 0, 0)),
        out_shape=jax.ShapeDtypeStruct((bsz, t, H_B * KV_LORA), BF16),
        compiler_params=pltpu.CompilerParams(
            dimension_semantics=("parallel",), vmem_limit_bytes=VMEM_LIMIT),
        name="mla_sample_attn",
    )(*attn_args)
    (y,) = _one_step_call(
        _mla_sample_out_kernel, [x2, olat.reshape(n, -1), gate, wuv, wout, npost],
        [jax.ShapeDtypeStruct((n, d), F32)], "mla_sample_out")
    return y.reshape(bsz, t, d)


def _rope_tables(pos_offset, tlen):
    inv = jnp.power(ROPE_BASE, -jnp.arange(HALF, dtype=F32) / HALF)
    ang = (pos_offset + jnp.arange(tlen)).astype(F32)[:, None] * inv[None, :]
    cos = jnp.cos(ang)
    sin = jnp.sin(ang)
    return jnp.concatenate([cos, cos], axis=1), jnp.concatenate([sin, sin], axis=1)


def _rot_cols(w):
    return jnp.concatenate([-w[..., HALF:], w[..., :HALF]], axis=-1)


def _row(a):
    return a.reshape(1, -1)


def kernel(x_prompt, x_sample, state_gla, cache_ckv, cache_krope, a_norm_pre, a_w_in, a_w_gate2, a_b_gate,
           a_head_norm, a_w_out, a_norm_post, kv_norm, kv_w_down, kv_latent_norm, kv_w_uk, kv_w_uv,
           b_norm_pre, b_w_in, b_q_norm, b_w_uq, b_w_out, b_norm_post):
    past = cache_ckv.shape[1]
    t_p, t_s = x_prompt.shape[1], x_sample.shape[1]

    a_win = a_w_in.astype(BF16)
    a_wglr = jnp.pad(a_w_in[:, :, 2 * KD_A + 2 * VD_A:],
                     ((0, 0), (0, 0), (0, 2 * LANES - GATE_RANK))).astype(BF16)
    a_wg2 = a_w_gate2.astype(BF16)
    a_wout = a_w_out.astype(BF16)
    wdc = kv_w_down[:, :KV_LORA].astype(BF16)
    wdr_cols = kv_w_down[:, KV_LORA:]
    wdr = jnp.concatenate([wdr_cols, wdr_cols, _rot_cols(wdr_cols), _rot_cols(wdr_cols)], axis=1).astype(BF16)
    wuk = kv_w_uk.astype(BF16)
    wuv = kv_w_uv.astype(BF16)
    b_win = b_w_in.astype(BF16)
    b_wout = b_w_out.astype(BF16)
    uq = b_w_uq.reshape(N_B, Q_LORA, H_B, NOPE + ROPE)
    uq_nope = uq[..., :NOPE].reshape(N_B, Q_LORA, H_B * NOPE)
    uq_rope = uq[..., NOPE:]
    uq_rot = _rot_cols(uq_rope)
    wqt_p = jnp.concatenate([uq_nope, uq_rope.reshape(N_B, Q_LORA, H_B * ROPE)],
                            axis=-1).astype(BF16).transpose(0, 2, 1)
    wuvt = wuv.T
    pad = lambda w: jnp.pad(w, ((0, 0), (0, 0), (0, 0), (0, LANES - ROPE))).reshape(N_B, Q_LORA, H_B * LANES)
    wq_s = jnp.concatenate([uq_nope, pad(uq_rope), pad(uq_rot)], axis=-1).astype(BF16)

    cos_p, sin_p = _rope_tables(0, t_p)
    cos_s, sin_s = _rope_tables(past, t_s)
    tile2 = lambda a: jnp.concatenate([a, a], axis=1)
    cos8t_p, sin8t_p = cos_p[:, :HALF].T, sin_p[:, :HALF].T
    padl = lambda a: jnp.tile(jnp.pad(a, ((0, 0), (0, LANES - ROPE))), (1, H_B))
    cos8_s, sin8_s = padl(cos_s), padl(sin_s)

    def lay(a, l):
        return _Layer(a.reshape(a.shape[0], 1, -1) if a.ndim == 2 else a, l)

    def gla_weights(l):
        return (lay(a_norm_pre, l), lay(a_win, l), lay(a_wglr, l), lay(a_wg2, l), lay(a_b_gate, l),
                lay(a_head_norm, l), lay(a_wout, l), lay(a_norm_post, l))

    assert t_p // CHUNK <= LANES - ROPE
    koh = jax.nn.one_hot(ROPE + jnp.arange(t_p) // CHUNK, LANES, dtype=F32)
    kv_args = (tile2(cos_p), tile2(sin_p), koh, _row(kv_norm), wdc, wdr, _row(kv_latent_norm), wuk, wuvt)
    x, st_p = x_prompt, None
    for l in range(N_A - 1):
        x, st_p = _gla_prompt_layer(x, st_p, gla_weights(l), rows=GLA_ROWS, chunk=CHUNK)
    x, st_p, ckv_p, kr_p, kn, kr0, vt = _gla_prompt_layer(
        x, st_p, gla_weights(N_A - 1), kv_args, rows=GLA_ROWS, chunk=CHUNK, kv_blk=ATT_K)
    for j in range(N_B):
        x = _mla_prompt_layer(x, kn, kr0, vt, cos8t_p, sin8t_p, lay(b_norm_pre, j), lay(b_win, j),
                              lay(b_q_norm, j), lay(wqt_p, j), lay(b_wout, j), lay(b_norm_post, j),
                              tq=ATT_Q, tk=ATT_K)
    y_prompt = x

    assert t_s <= CHUNK
    bsz_s = x_sample.shape[0]
    x, states = x_sample, []
    for l in range(N_A):
        x, s_new = _gla_sample_layer(x, state_gla, l, gla_weights(l), groups=SAMPLE_GROUPS)
        states.append(s_new)
    st_s = jnp.stack(states)
    per_row = lambda a: jnp.tile(tile2(a), (bsz_s, 1))
    ckv_s, kr_s = _shared_kv_latent(x.reshape(bsz_s * t_s, D_MODEL), per_row(cos_s), per_row(sin_s),
                                    _row(kv_norm), wdc, wdr, _row(kv_latent_norm))
    ckv_s = ckv_s.reshape(bsz_s, t_s, KV_LORA)
    kr_s = kr_s.reshape(bsz_s, t_s, ROPE)
    cos_rows, sin_rows = jnp.tile(cos8_s, (bsz_s, 1)), jnp.tile(sin8_s, (bsz_s, 1))
    for j in range(N_B):
        x = _mla_sample_layer(x, cache_ckv, cache_krope, ckv_s, kr_s, cos_rows, sin_rows,
                              lay(b_norm_pre, j), lay(b_win, j), lay(b_q_norm, j), lay(wq_s, j), wuk, wuv,
                              lay(b_wout, j), lay(b_norm_post, j))
    y_sample = x

    return (y_prompt, y_sample, st_p, st_s, ckv_p, kr_p, ckv_s, kr_s)
```

```python
import functools

import jax
import jax.numpy as jnp
from jax import lax
from jax.experimental import pallas as pl
from jax.experimental.pallas import tpu as pltpu

F32 = jnp.float32
BF16 = jnp.bfloat16

D_MODEL = 1024
CHUNK = 64
N_A = 2
N_B = 2
H_A = 4
DK_A = 128
DV_A = 256
KD_A = H_A * DK_A
VD_A = H_A * DV_A
GATE_RANK = 16
GATE_TAU = 16.0
H_B = 8
NOPE = 128
ROPE = 64
HALF = ROPE // 2
V_B = 128
VD_B = H_B * V_B
Q_LORA = 384
KV_LORA = 256
ROPE_BASE = 10000.0
EPS = 1e-6

LANES = 128
VMEM_LIMIT = 56 * 1024 * 1024

GLA_ROWS = 512
GLA_SUB = 256
SAMPLE_ATTN_SEQS = 4
SAMPLE_GROUPS = 2
ATT_Q = 512
ATT_K = 256
LOG2E = 1.4426950408889634
MASKED = -1e30


def _dot(a, b):
    return jnp.dot(a, b, preferred_element_type=F32)


def _dot_nt(a, b):
    return lax.dot_general(a, b, (((1,), (1,)), ((), ())), preferred_element_type=F32)


def _dot_tn(a, b):
    return lax.dot_general(a, b, (((0,), (0,)), ((), ())), preferred_element_type=F32)


def _rms(xf, g):
    ms = jnp.mean(xf * xf, axis=-1, keepdims=True)
    return xf * lax.rsqrt(ms + EPS) * g


def _silu(g):
    h = 0.5 * g
    return h + h * jnp.tanh(h)


def _const_spec(shape):
    return pl.BlockSpec(shape, lambda *_: (0,) * len(shape))


class _Layer:
    def __init__(self, stacked, l):
        self.stacked, self.l, self.shape = stacked, l, stacked.shape[1:]


def _weight_spec(w):
    if isinstance(w, _Layer):
        l, nd = w.l, len(w.shape)
        return pl.BlockSpec((None,) + w.shape, lambda *_: (l,) + (0,) * nd)
    return _const_spec(w.shape)


def _weight_arg(w):
    return w.stacked if isinstance(w, _Layer) else w


def _kv_down(xb, nkv_ref, wdc_ref, wdr_ref):
    hb = _rms(xb, nkv_ref[...]).astype(BF16)
    return _dot(hb, wdc_ref[...]), _dot(hb, wdr_ref[...])


def _kv_finish(lat, rr, cos, sin, nlat_ref):
    return _rms(lat, nlat_ref[...]), rr[:, :LANES] * cos + rr[:, LANES:] * sin


def _gla_kernel(*refs, rows, chunk, carry, n_prev, kv_blk):
    refs = list(refs)
    x_ref = refs.pop(0)
    s0_ref = None if carry else refs.pop(0)
    prev_ref = refs.pop(0) if n_prev else None
    npre_ref, win_ref, wglr_ref, wg2_ref, bg_ref, hn_ref, wout_ref, npost_ref = refs[:8]
    del refs[:8]
    if kv_blk:
        (cos_ref, sin_ref, koh_ref, nkv_ref, wdc_ref, wdr_ref, nlat_ref, wuk_ref, wuvt_ref) = refs[:9]
        del refs[:9]
    y_ref, sout_ref = refs[:2]
    del refs[:2]
    if kv_blk:
        ckv_ref, kr_ref, kn_ref, kr0_ref, vt_ref = refs[:5]
        del refs[:5]
    st_sc = refs.pop(0) if carry else None
    o_sc, qe_sc, u_sc, stb_sc = refs
    t = pl.program_id(1)

    if carry:
        @pl.when(t == 0)
        def _():
            st_sc[...] = jnp.zeros(st_sc.shape, F32)

    shift = chunk.bit_length() - 1
    sub = min(rows, GLA_SUB)
    blocks = range(0, rows, sub)
    xs, z, proj, b = {}, {}, {}, {}
    n_main = 2 * KD_A + 2 * VD_A
    for r in blocks:
        xs[r] = x_ref[0, r:r + sub, :]
        hb = _rms(xs[r], npre_ref[...]).astype(BF16)
        glr = _dot(hb, wglr_ref[...])[:, :GATE_RANK].astype(BF16)
        proj_qk = _dot(hb, win_ref[:, :2 * KD_A])
        z[r] = _dot(glr, wg2_ref[...]) + bg_ref[...]
        proj[r] = jnp.concatenate([proj_qk, _dot(hb, win_ref[:, 2 * KD_A:n_main])], axis=1)

    ri = lax.broadcasted_iota(jnp.int32, (sub, sub), 0)
    ci = lax.broadcasted_iota(jnp.int32, (sub, sub), 1)
    inchunk = jnp.where(jnp.right_shift(ri, shift) == jnp.right_shift(ci, shift), ci, sub) <= ri
    tri = jnp.where(inchunk, 1.0, 0.0).astype(BF16)
    for r in blocks:
        g = (jnp.minimum(z[r], 0.0) - jnp.log(1.0 + jnp.exp(-jnp.abs(z[r])))) * (LOG2E / GATE_TAU)
        g_hi = g.astype(BF16)
        g_lo = (g - g_hi.astype(F32)).astype(BF16)
        b[r] = _dot(tri, g_hi) + _dot(tri, g_lo)

    scale = DK_A ** -0.5
    nc = rows // chunk
    decay = [[None] * nc for _ in range(H_A)]

    for h in range(H_A):
        for r in blocks:
            chunks = range(r // chunk, (r + sub) // chunk)
            bh = b[r][:, h * DK_A:(h + 1) * DK_A]
            ends = {c: bh[(c + 1) * chunk - 1 - r:(c + 1) * chunk - r, :] for c in chunks}
            bl = jnp.concatenate([jnp.broadcast_to(ends[c], (chunk, DK_A)) for c in chunks], axis=0)
            qh = proj[r][:, h * DK_A:(h + 1) * DK_A]
            kh = proj[r][:, KD_A + h * DK_A:KD_A + (h + 1) * DK_A]
            vh = proj[r][:, 2 * KD_A + h * DV_A:2 * KD_A + (h + 1) * DV_A].astype(BF16)
            qe = (qh * scale * jnp.exp2(bh)).astype(BF16)
            ke = (kh * jnp.exp2(-bh)).astype(BF16)
            kd = (kh * jnp.exp2(bl - bh)).astype(BF16)
            s_raw = _dot_nt(qe, ke)
            qe_sc[r:r + sub, h * DK_A:(h + 1) * DK_A] = qe
            for c in chunks:
                r0, r1 = c * chunk - r, (c + 1) * chunk - r
                u_sc[h, c] = _dot_tn(kd[r0:r1], vh[r0:r1])
                dcol = jnp.broadcast_to(jnp.exp2(ends[c]), (LANES, DK_A)).T
                decay[h][c] = jnp.concatenate([dcol] * (DV_A // LANES), axis=1)
            s = jnp.where(inchunk, s_raw, 0.0).astype(BF16)
            o_sc[r:r + sub, h * DV_A:(h + 1) * DV_A] = _dot(s, vh)

    for h in range(H_A):
        if carry:
            st = st_sc[h]
        for c in range(nc):
            if not carry:
                st = s0_ref[c, h]
            stb_sc[h, c] = st.astype(BF16)
            st_new = st * decay[h][c] + u_sc[h, c]
            if carry:
                st = st_new
            else:
                sout_ref[c, h] = st_new
        if carry:
            st_sc[h] = st
        for c in range(nc):
            r0, r1 = c * chunk, (c + 1) * chunk
            o_sc[r0:r1, h * DV_A:(h + 1) * DV_A] += _dot(qe_sc[r0:r1, h * DK_A:(h + 1) * DK_A], stb_sc[h, c])

    x_new = {}
    for r in blocks:
        gate = proj[r][:, 2 * KD_A + VD_A:2 * KD_A + 2 * VD_A]
        heads = []
        for h in range(H_A):
            heads.append(_rms(o_sc[r:r + sub, h * DV_A:(h + 1) * DV_A], hn_ref[:, h * DV_A:(h + 1) * DV_A]))
        og = (jnp.concatenate(heads, axis=1) * _silu(gate)).astype(BF16)
        y = _dot(og, wout_ref[...])
        x_new[r] = xs[r] + _rms(y, npost_ref[...])
        y_ref[0, r:r + sub, :] = x_new[r]

    if kv_blk:
        down = {r: _kv_down(x_new[r], nkv_ref, wdc_ref, wdr_ref) for r in blocks}
        for r in blocks:
            ckv, kr2 = _kv_finish(*down[r], cos_ref[r:r + sub, :], sin_ref[r:r + sub, :], nlat_ref)
            ckv_ref[0, r:r + sub, :] = ckv
            kr_ref[0, r:r + sub, :] = kr2[:, :ROPE]
            cb = ckv.astype(BF16)
            kn_ref[0, r:r + sub, :] = _dot(cb, wuk_ref[...]).astype(BF16)
            lane = lax.broadcasted_iota(jnp.int32, kr2.shape, 1)
            kr0_ref[0, r:r + sub, :] = jnp.where(lane < ROPE, kr2, koh_ref[r:r + sub, :]).astype(BF16)
            vt = _dot_nt(wuvt_ref[...], cb).astype(BF16)
            for i in range(sub // kv_blk):
                vt_ref[0, r // kv_blk + i] = vt[:, i * kv_blk:(i + 1) * kv_blk]

    if carry:
        @pl.when(t == pl.num_programs(1) - 1)
        def _():
            for li in range(n_prev):
                sout_ref[li, 0] = prev_ref[li, 0]
            for h in range(H_A):
                sout_ref[n_prev, 0, h] = st_sc[h]


def _gla_scratch(rows, chunk):
    return [pltpu.VMEM((rows, VD_A), F32), pltpu.VMEM((rows, KD_A), BF16),
            pltpu.VMEM((H_A, rows // chunk, DK_A, DV_A), F32),
            pltpu.VMEM((H_A, rows // chunk, DK_A, DV_A), BF16)]


def _gla_prompt_layer(x, prev, weights, kv=None, *, rows, chunk, kv_blk=None):
    bsz, tlen, d = x.shape
    n_prev = 0 if prev is None else prev.shape[0]
    tok = lambda w: pl.BlockSpec((1, rows, w), lambda b, t: (b, t, 0))
    st_spec = lambda n: pl.BlockSpec((n, 1, H_A, DK_A, DV_A), lambda b, t: (0, b, 0, 0, 0))
    in_specs = [tok(d)] + ([st_spec(n_prev)] if n_prev else []) + [_weight_spec(a) for a in weights]
    args = [x] + ([prev] if n_prev else []) + [_weight_arg(a) for a in weights]
    out_specs = [tok(d), st_spec(n_prev + 1)]
    out_shape = [jax.ShapeDtypeStruct((bsz, tlen, d), F32),
                 jax.ShapeDtypeStruct((n_prev + 1, bsz, H_A, DK_A, DV_A), F32)]
    if kv is not None:
        assert min(rows, GLA_SUB) % kv_blk == 0
        tables, kv_weights = kv[:3], kv[3:]
        tab = pl.BlockSpec((rows, LANES), lambda b, t: (t, 0))
        in_specs += [tab] * 3 + [_const_spec(a.shape) for a in kv_weights]
        args += list(tables) + list(kv_weights)
        out_specs += [tok(KV_LORA), tok(ROPE), tok(H_B * NOPE), tok(LANES),
                      pl.BlockSpec((1, rows // kv_blk, VD_B, kv_blk), lambda b, t: (b, t, 0, 0))]
        out_shape += [jax.ShapeDtypeStruct((bsz, tlen, KV_LORA), F32),
                      jax.ShapeDtypeStruct((bsz, tlen, ROPE), F32),
                      jax.ShapeDtypeStruct((bsz, tlen, H_B * NOPE), BF16),
                      jax.ShapeDtypeStruct((bsz, tlen, LANES), BF16),
                      jax.ShapeDtypeStruct((bsz, tlen // kv_blk, VD_B, kv_blk), BF16)]
    return pl.pallas_call(
        functools.partial(_gla_kernel, rows=rows, chunk=chunk, carry=True, n_prev=n_prev,
                          kv_blk=kv_blk if kv is not None else None),
        grid=(bsz, tlen // rows),
        in_specs=in_specs,
        out_specs=tuple(out_specs),
        out_shape=tuple(out_shape),
        scratch_shapes=[pltpu.VMEM((H_A, DK_A, DV_A), F32)] + _gla_scratch(rows, chunk),
        compiler_params=pltpu.CompilerParams(
            dimension_semantics=("parallel", "arbitrary"), vmem_limit_bytes=VMEM_LIMIT),
        name="gla_layer",
    )(*args)


def _gla_sample_layer(x, states, layer, weights, *, groups):
    bsz, t, d = x.shape
    per = bsz // groups
    rows = per * t
    tok = pl.BlockSpec((1, rows, d), lambda g, _: (g, 0, 0))
    st_spec = pl.BlockSpec((per, H_A, DK_A, DV_A), lambda g, _: (g, 0, 0, 0))
    st_in = pl.BlockSpec((None, per, H_A, DK_A, DV_A), lambda g, _: (layer, g, 0, 0, 0))
    y, s_new = pl.pallas_call(
        functools.partial(_gla_kernel, rows=rows, chunk=t, carry=False, n_prev=0, kv_blk=None),
        grid=(groups, 1),
        in_specs=[tok, st_in] + [_weight_spec(a) for a in weights],
        out_specs=(tok, st_spec),
        out_shape=(jax.ShapeDtypeStruct((groups, rows, d), F32),
                   jax.ShapeDtypeStruct((bsz, H_A, DK_A, DV_A), F32)),
        scratch_shapes=_gla_scratch(rows, t),
        compiler_params=pltpu.CompilerParams(
            dimension_semantics=("parallel", "arbitrary"), vmem_limit_bytes=VMEM_LIMIT),
        name="gla_sample_layer",
    )(x.reshape(groups, rows, d), states, *[_weight_arg(a) for a in weights])
    return y.reshape(bsz, t, d), s_new


def _kv_kernel(x_ref, cos_ref, sin_ref, nkv_ref, wdc_ref, wdr_ref, nlat_ref, ckv_ref, kr_ref):
    lat, rr = _kv_down(x_ref[...], nkv_ref, wdc_ref, wdr_ref)
    ckv, kr2 = _kv_finish(lat, rr, cos_ref[...], sin_ref[...], nlat_ref)
    ckv_ref[...] = ckv
    kr_ref[...] = kr2[:, :ROPE]


def _shared_kv_latent(x, cos2, sin2, nkv, wdc, wdr, nlat):
    n, d = x.shape
    args = [x, cos2, sin2, nkv, wdc, wdr, nlat]
    return pl.pallas_call(
        _kv_kernel,
        grid=(1,),
        in_specs=[_const_spec(a.shape) for a in args],
        out_specs=(_const_spec((n, KV_LORA)), _const_spec((n, ROPE))),
        out_shape=(jax.ShapeDtypeStruct((n, KV_LORA), F32), jax.ShapeDtypeStruct((n, ROPE), F32)),
        compiler_params=pltpu.CompilerParams(
            dimension_semantics=("arbitrary",), vmem_limit_bytes=VMEM_LIMIT),
        name="shared_kv",
    )(*args)


def _mla_q(x, cos, sin, npre_ref, win_ref, qn_ref, wq_ref, rope_w):
    hb = _rms(x, npre_ref[...]).astype(BF16)
    proj = _dot(hb, win_ref[...])
    qn = _rms(proj[:, :Q_LORA], qn_ref[...]).astype(BF16)
    qq = _dot(qn, wq_ref[...])
    scale = (NOPE + ROPE) ** -0.5
    n0, n1 = H_B * NOPE, H_B * NOPE + H_B * rope_w
    q_nope = (qq[:, :n0] * scale).astype(BF16)
    q_rope = ((qq[:, n0:n1] * cos + qq[:, n1:] * sin) * scale).astype(BF16)
    return q_nope, q_rope, proj[:, Q_LORA:]


ACC_ROWS = V_B + 16


def _mla_kernel(x_ref, kn_ref, kr_ref, vt_ref, cos_ref, sin_ref, npre_ref, win_ref, qn_ref, wqt_ref,
                wout_ref, npost_ref, y_ref, q_sc, m_sc, acc_sc, alpha_sc, p_sc, *, tq, tk):
    i = pl.program_id(1)
    c = (NOPE + ROPE) ** -0.5 * LOG2E
    n0, n1 = H_B * NOPE, H_B * (NOPE + ROPE)
    x = x_ref[0]
    hbs = {r: _rms(x[r:r + tk], npre_ref[...]).astype(BF16) for r in range(0, tq, tk)}
    qlat = jnp.concatenate([_dot(hbs[r], win_ref[:, :Q_LORA]) for r in hbs], axis=0)
    gates = {r: _dot(hbs[r], win_ref[:, Q_LORA:]) for r in hbs}
    qnt = _rms(qlat, qn_ref[...]).T.astype(BF16)
    qqt = _dot(wqt_ref[...], qnt)
    q_sc[:n0, :] = (qqt[:n0] * c).astype(BF16)
    cos, sin = cos_ref[...], sin_ref[...]
    for h in range(H_B):
        r0 = n0 + h * ROPE
        x1, x2 = qqt[r0:r0 + HALF], qqt[r0 + HALF:r0 + ROPE]
        q_sc[r0:r0 + HALF, :] = ((x1 * cos - x2 * sin) * c).astype(BF16)
        q_sc[r0 + HALF:r0 + ROPE, :] = ((x2 * cos + x1 * sin) * c).astype(BF16)
    row = lax.broadcasted_iota(jnp.int32, (ROPE, tq), 0)
    q_chunk = jnp.right_shift(i * tq + lax.broadcasted_iota(jnp.int32, (ROPE, tq), 1), CHUNK.bit_length() - 1)
    q_sc[n1:, :] = jnp.where(row > q_chunk, MASKED, 0.0).astype(BF16)
    m_sc[...] = jnp.full(m_sc.shape, -jnp.inf, F32)
    acc_sc[...] = jnp.zeros(acc_sc.shape, F32)

    ones = jnp.ones((ACC_ROWS - V_B, tk), BF16)

    def scores(j, h, lo, hi):
        k0 = pl.multiple_of(j * tk, tk)
        kb = jnp.concatenate(
            [kn_ref[0, pl.ds(k0, tk), h * NOPE:(h + 1) * NOPE], kr_ref[0, pl.ds(k0, tk), :]], axis=1)
        qt = jnp.concatenate(
            [q_sc[h * NOPE:(h + 1) * NOPE, lo:hi], q_sc[n0 + h * ROPE:n0 + (h + 1) * ROPE, lo:hi],
             q_sc[n1:, lo:hi]], axis=0)
        return _dot(kb, qt)

    def softmax(s, h, slot, lo, hi):
        m_prev = m_sc[h, :, lo:hi]
        m_new = jnp.maximum(m_prev, jnp.max(s, axis=0, keepdims=True))
        alpha_sc[slot, h, :, lo:hi] = jnp.exp2(m_prev - m_new)
        p_sc[slot, h, :, lo:hi] = jnp.exp2(s - m_new).astype(BF16)
        m_sc[h, :, lo:hi] = m_new

    def values(j, h, slot, lo, hi):
        vt = jnp.concatenate([vt_ref[0, j, h * V_B:(h + 1) * V_B, :], ones], axis=0)
        acc_sc[h, :, lo:hi] = (alpha_sc[slot, h, :, lo:hi] * acc_sc[h, :, lo:hi]
                               + _dot(vt, p_sc[slot, h, :, lo:hi]))

    for h in range(H_B):
        softmax(scores(0, h, 0, tq), h, 0, 0, tq)

    def block(j, slot_new, slot_old, lo=0):
        s = [scores(j, h, lo, tq) for h in range(H_B)]
        for h in range(H_B):
            values(j - 1, h, slot_old, 0, tq)
        for h in range(H_B):
            softmax(s[h], h, slot_new, lo, tq)

    def pair(jj, carry):
        block(2 * jj + 1, 1, 0)
        block(2 * jj + 2, 0, 1)
        return carry

    def tail_odd(last, lo=0):
        block(last, 1, 0, lo)
        for h in range(H_B):
            values(last, h, 1, lo, tq)

    def tail_even(last):
        for h in range(H_B):
            values(last, h, 0, 0, tq)

    if tq == 2 * tk:
        lax.fori_loop(0, i, pair, 0)
        tail_odd(2 * i + 1, tk)
    else:
        lax.fori_loop(0, i // 2, pair, 0)
        odd = lax.rem(i, 2) == 1
        pl.when(odd)(lambda: tail_odd(i))
        pl.when(jnp.logical_not(odd))(lambda: tail_even(i))

    for r in range(0, tq, tk):
        gate = gates[r]
        outs = []
        for h in range(H_B):
            a = acc_sc[h, :, r:r + tk]
            outs.append(a[:V_B] * (1.0 / a[V_B:V_B + 1]))
        o = jnp.concatenate(outs, axis=0).T
        og = (o * _silu(gate)).astype(BF16)
        y = _dot(og, wout_ref[...])
        y_ref[0, r:r + tk, :] = x[r:r + tk] + _rms(y, npost_ref[...])


def _mla_prompt_layer(x, kn, kr0, vt, cos8t, sin8t, npre, win, qn, wqt, wout, npost, *, tq, tk):
    bsz, tlen, d = x.shape
    assert tq in (tk, 2 * tk) and tk % CHUNK == 0 and vt.shape[3] == tk
    tok = pl.BlockSpec((1, tq, d), lambda b, t: (b, t, 0))
    full = lambda w: pl.BlockSpec((1, tlen, w), lambda b, t: (b, 0, 0))
    tab = pl.BlockSpec((cos8t.shape[0], tq), lambda b, t: (0, t))
    weights = [npre, win, qn, wqt, wout, npost]
    return pl.pallas_call(
        functools.partial(_mla_kernel, tq=tq, tk=tk),
        grid=(bsz, tlen // tq),
        in_specs=[tok, full(kn.shape[2]), full(kr0.shape[2]),
                  pl.BlockSpec((1,) + vt.shape[1:], lambda b, t: (b, 0, 0, 0)), tab, tab]
                 + [_weight_spec(a) for a in weights],
        out_specs=tok,
        out_shape=jax.ShapeDtypeStruct((bsz, tlen, d), F32),
        scratch_shapes=[pltpu.VMEM((H_B * (NOPE + ROPE) + ROPE, tq), BF16),
                        pltpu.VMEM((H_B, 1, tq), F32),
                        pltpu.VMEM((H_B, ACC_ROWS, tq), F32),
                        pltpu.VMEM((2, H_B, 1, tq), F32),
                        pltpu.VMEM((2, H_B, tk, tq), BF16)],
        compiler_params=pltpu.CompilerParams(
            dimension_semantics=("parallel", "arbitrary"), vmem_limit_bytes=VMEM_LIMIT),
        name="mla_prompt_layer",
    )(x, kn, kr0, vt, cos8t, sin8t, *[_weight_arg(a) for a in weights])


def _mla_sample_q_kernel(x_ref, cos_ref, sin_ref, npre_ref, win_ref, qn_ref, wq_ref, wuk_ref,
                         qlat_ref, qr_ref, gate_ref):
    q_nope, q_rope, gate = _mla_q(x_ref[...], cos_ref[...], sin_ref[...], npre_ref, win_ref, qn_ref,
                                  wq_ref, LANES)
    for h in range(H_B):
        qlat_ref[:, h * KV_LORA:(h + 1) * KV_LORA] = _dot_nt(
            q_nope[:, h * NOPE:(h + 1) * NOPE], wuk_ref[:, h * NOPE:(h + 1) * NOPE]).astype(BF16)
    qr_ref[...] = q_rope
    gate_ref[...] = gate


def _mla_sample_out_kernel(x_ref, olat_ref, gate_ref, wuv_ref, wout_ref, npost_ref, y_ref):
    o = jnp.concatenate(
        [_dot(olat_ref[:, h * KV_LORA:(h + 1) * KV_LORA], wuv_ref[:, h * V_B:(h + 1) * V_B])
         for h in range(H_B)], axis=1)
    og = (o * _silu(gate_ref[...])).astype(BF16)
    y = _dot(og, wout_ref[...])
    y_ref[...] = x_ref[...] + _rms(y, npost_ref[...])


def _mla_sample_attn_kernel(qlat_ref, qr_ref, cckv_ref, ckr_ref, nckv_ref, nkr_ref, olat_ref, *, past):
    nseq, t = qlat_ref.shape[0], qlat_ref.shape[1]
    cshift = CHUNK.bit_length() - 1

    def chunk_mask(s, k_start):
        rows = lax.broadcasted_iota(jnp.int32, (s.shape[0], 1), 0)
        cols = lax.broadcasted_iota(jnp.int32, (1, s.shape[1]), 1)
        q_chunk = jnp.right_shift(past + jnp.bitwise_and(rows, t - 1), cshift)
        k_chunk = jnp.right_shift(k_start + cols, cshift)
        return jnp.where(k_chunk <= q_chunk, s, -jnp.inf)

    c_ckv, n_ckv, s_c, s_n = {}, {}, {}, {}
    for i in range(nseq):
        qlat = jnp.concatenate([qlat_ref[i, :, h * KV_LORA:(h + 1) * KV_LORA] for h in range(H_B)], axis=0)
        qr = jnp.concatenate([qr_ref[i, :, h * LANES:h * LANES + ROPE] for h in range(H_B)], axis=0)
        c_ckv[i] = cckv_ref[i].astype(BF16)
        n_ckv[i] = nckv_ref[i].astype(BF16)
        s_c[i] = _dot_nt(qlat, c_ckv[i]) + _dot_nt(qr, ckr_ref[i].astype(BF16))
        s_n[i] = _dot_nt(qlat, n_ckv[i]) + _dot_nt(qr, nkr_ref[i].astype(BF16))
    for i in range(nseq):
        sc, sn = chunk_mask(s_c[i], 0), chunk_mask(s_n[i], past)
        m = jnp.maximum(jnp.max(sc, axis=-1, keepdims=True), jnp.max(sn, axis=-1, keepdims=True))
        p_c = jnp.exp(sc - m)
        p_n = jnp.exp(sn - m)
        l = jnp.sum(p_c, axis=-1, keepdims=True) + jnp.sum(p_n, axis=-1, keepdims=True)
        olat = ((_dot(p_c.astype(BF16), c_ckv[i]) + _dot(p_n.astype(BF16), n_ckv[i])) * (1.0 / l)).astype(BF16)
        for h in range(H_B):
            olat_ref[i, :, h * KV_LORA:(h + 1) * KV_LORA] = olat[h * t:(h + 1) * t]


def _one_step_call(body, args, out_shapes, name):
    return pl.pallas_call(
        body,
        grid=(1,),
        in_specs=[_weight_spec(a) for a in args],
        out_specs=tuple(_const_spec(s.shape) for s in out_shapes),
        out_shape=tuple(out_shapes),
        compiler_params=pltpu.CompilerParams(
            dimension_semantics=("arbitrary",), vmem_limit_bytes=VMEM_LIMIT),
        name=name,
    )(*[_weight_arg(a) for a in args])


def _mla_sample_layer(x, cache_ckv, cache_kr, new_ckv, new_kr, cos, sin, npre, win, qn, wq, wuk, wuv,
                      wout, npost):
    bsz, t, d = x.shape
    past = cache_ckv.shape[1]
    assert t & (t - 1) == 0
    n = bsz * t
    x2 = x.reshape(n, d)
    qlat, qr, gate = _one_step_call(
        _mla_sample_q_kernel, [x2, cos, sin, npre, win, qn, wq, wuk],
        [jax.ShapeDtypeStruct((n, H_B * KV_LORA), BF16), jax.ShapeDtypeStruct((n, H_B * LANES), BF16),
         jax.ShapeDtypeStruct((n, VD_B), F32)], "mla_sample_q")
    nseq = SAMPLE_ATTN_SEQS if bsz % SAMPLE_ATTN_SEQS == 0 else 1
    per_b = lambda a: pl.BlockSpec((nseq,) + a.shape[1:], lambda b: (b, 0, 0))
    attn_args = [qlat.reshape(bsz, t, -1), qr.reshape(bsz, t, -1), cache_ckv, cache_kr, new_ckv, new_kr]
    olat = pl.pallas_call(
        functools.partial(_mla_sample_attn_kernel, past=past),
        grid=(bsz // nseq,),
        in_specs=[per_b(a) for a in attn_args],
        out_specs=pl.BlockSpec((nseq, t, H_B * KV_LORA), lambda b: (b, 0, 0)),
        out_shape=jax.ShapeDtypeStruct((bsz, t, H_B * KV_LORA), BF16),
        compiler_params=pltpu.CompilerParams(
            dimension_semantics=("parallel",), vmem_limit_bytes=VMEM_LIMIT),
        name="mla_sample_attn",
    )(*attn_args)
    (y,) = _one_step_call(
        _mla_sample_out_kernel, [x2, olat.reshape(n, -1), gate, wuv, wout, npost],
        [jax.ShapeDtypeStruct((n, d), F32)], "mla_sample_out")
    return y.reshape(bsz, t, d)


def _rope_tables(pos_offset, tlen):
    inv = jnp.power(ROPE_BASE, -jnp.arange(HALF, dtype=F32) / HALF)
    ang = (pos_offset + jnp.arange(tlen)).astype(F32)[:, None] * inv[None, :]
    cos = jnp.cos(ang)
    sin = jnp.sin(ang)
    return jnp.concatenate([cos, cos], axis=1), jnp.concatenate([sin, sin], axis=1)


def _rot_cols(w):
    return jnp.concatenate([-w[..., HALF:], w[..., :HALF]], axis=-1)


def _row(a):
    return a.reshape(1, -1)


def kernel(x_prompt, x_sample, state_gla, cache_ckv, cache_krope, a_norm_pre, a_w_in, a_w_gate2, a_b_gate,
           a_head_norm, a_w_out, a_norm_post, kv_norm, kv_w_down, kv_latent_norm, kv_w_uk, kv_w_uv,
           b_norm_pre, b_w_in, b_q_norm, b_w_uq, b_w_out, b_norm_post):
    past = cache_ckv.shape[1]
    t_p, t_s = x_prompt.shape[1], x_sample.shape[1]

    a_win = a_w_in.astype(BF16)
    a_wglr = jnp.pad(a_w_in[:, :, 2 * KD_A + 2 * VD_A:],
                     ((0, 0), (0, 0), (0, 2 * LANES - GATE_RANK))).astype(BF16)
    a_wg2 = a_w_gate2.astype(BF16)
    a_wout = a_w_out.astype(BF16)
    wdc = kv_w_down[:, :KV_LORA].astype(BF16)
    wdr_cols = kv_w_down[:, KV_LORA:]
    wdr = jnp.concatenate([wdr_cols, wdr_cols, _rot_cols(wdr_cols), _rot_cols(wdr_cols)], axis=1).astype(BF16)
    wuk = kv_w_uk.astype(BF16)
    wuv = kv_w_uv.astype(BF16)
    b_win = b_w_in.astype(BF16)
    b_wout = b_w_out.astype(BF16)
    uq = b_w_uq.reshape(N_B, Q_LORA, H_B, NOPE + ROPE)
    uq_nope = uq[..., :NOPE].reshape(N_B, Q_LORA, H_B * NOPE)
    uq_rope = uq[..., NOPE:]
    uq_rot = _rot_cols(uq_rope)
    wqt_p = jnp.concatenate([uq_nope, uq_rope.reshape(N_B, Q_LORA, H_B * ROPE)],
                            axis=-1).astype(BF16).transpose(0, 2, 1)
    wuvt = wuv.T
    pad = lambda w: jnp.pad(w, ((0, 0), (0, 0), (0, 0), (0, LANES - ROPE))).reshape(N_B, Q_LORA, H_B * LANES)
    wq_s = jnp.concatenate([uq_nope, pad(uq_rope), pad(uq_rot)], axis=-1).astype(BF16)

    cos_p, sin_p = _rope_tables(0, t_p)
    cos_s, sin_s = _rope_tables(past, t_s)
    tile2 = lambda a: jnp.concatenate([a, a], axis=1)
    cos8t_p, sin8t_p = cos_p[:, :HALF].T, sin_p[:, :HALF].T
    padl = lambda a: jnp.tile(jnp.pad(a, ((0, 0), (0, LANES - ROPE))), (1, H_B))
    cos8_s, sin8_s = padl(cos_s), padl(sin_s)

    def lay(a, l):
        return _Layer(a.reshape(a.shape[0], 1, -1) if a.ndim == 2 else a, l)

    def gla_weights(l):
        return (lay(a_norm_pre, l), lay(a_win, l), lay(a_wglr, l), lay(a_wg2, l), lay(a_b_gate, l),
                lay(a_head_norm, l), lay(a_wout, l), lay(a_norm_post, l))

    assert t_p // CHUNK <= LANES - ROPE
    koh = jax.nn.one_hot(ROPE + jnp.arange(t_p) // CHUNK, LANES, dtype=F32)
    kv_args = (tile2(cos_p), tile2(sin_p), koh, _row(kv_norm), wdc, wdr, _row(kv_latent_norm), wuk, wuvt)
    x, st_p = x_prompt, None
    for l in range(N_A - 1):
        x, st_p = _gla_prompt_layer(x, st_p, gla_weights(l), rows=GLA_ROWS, chunk=CHUNK)
    x, st_p, ckv_p, kr_p, kn, kr0, vt = _gla_prompt_layer(
        x, st_p, gla_weights(N_A - 1), kv_args, rows=GLA_ROWS, chunk=CHUNK, kv_blk=ATT_K)
    for j in range(N_B):
        x = _mla_prompt_layer(x, kn, kr0, vt, cos8t_p, sin8t_p, lay(b_norm_pre, j), lay(b_win, j),
                              lay(b_q_norm, j), lay(wqt_p, j), lay(b_wout, j), lay(b_norm_post, j),
                              tq=ATT_Q, tk=ATT_K)
    y_prompt = x

    assert t_s <= CHUNK
    bsz_s = x_sample.shape[0]
    x, states = x_sample, []
    for l in range(N_A):
        x, s_new = _gla_sample_layer(x, state_gla, l, gla_weights(l), groups=SAMPLE_GROUPS)
        states.append(s_new)
    st_s = jnp.stack(states)
    per_row = lambda a: jnp.tile(tile2(a), (bsz_s, 1))
    ckv_s, kr_s = _shared_kv_latent(x.reshape(bsz_s * t_s, D_MODEL), per_row(cos_s), per_row(sin_s),
                                    _row(kv_norm), wdc, wdr, _row(kv_latent_norm))
    ckv_s = ckv_s.reshape(bsz_s, t_s, KV_LORA)
    kr_s = kr_s.reshape(bsz_s, t_s, ROPE)
    cos_rows, sin_rows = jnp.tile(cos8_s, (bsz_s, 1)), jnp.tile(sin8_s, (bsz_s, 1))
    for j in range(N_B):
        x = _mla_sample_layer(x, cache_ckv, cache_krope, ckv_s, kr_s, cos_rows, sin_rows,
                              lay(b_norm_pre, j), lay(b_win, j), lay(b_q_norm, j), lay(wq_s, j), wuk, wuv,
                              lay(b_wout, j), lay(b_norm_post, j))
    y_sample = x

    return (y_prompt, y_sample, st_p, st_s, ckv_p, kr_p, ckv_s, kr_s)
```

```python
import functools

import jax
import jax.numpy as jnp
from jax import lax
from jax.experimental import pallas as pl
from jax.experimental.pallas import tpu as pltpu

F32 = jnp.float32
BF16 = jnp.bfloat16

D_MODEL = 1024
CHUNK = 64
N_A = 2
N_B = 2
H_A = 4
DK_A = 128
DV_A = 256
KD_A = H_A * DK_A
VD_A = H_A * DV_A
GATE_RANK = 16
GATE_TAU = 16.0
H_B = 8
NOPE = 128
ROPE = 64
HALF = ROPE // 2
V_B = 128
VD_B = H_B * V_B
Q_LORA = 384
KV_LORA = 256
ROPE_BASE = 10000.0
EPS = 1e-6

LANES = 128
VMEM_LIMIT = 56 * 1024 * 1024

GLA_ROWS = 512
GLA_SUB = 256
SAMPLE_ATTN_SEQS = 4
SAMPLE_GROUPS = 2
ATT_Q = 512
ATT_K = 256
LOG2E = 1.4426950408889634
MASKED = -1e30


def _dot(a, b):
    return jnp.dot(a, b, preferred_element_type=F32)


def _dot_nt(a, b):
    return lax.dot_general(a, b, (((1,), (1,)), ((), ())), preferred_element_type=F32)


def _dot_tn(a, b):
    return lax.dot_general(a, b, (((0,), (0,)), ((), ())), preferred_element_type=F32)


def _rms(xf, g):
    ms = jnp.mean(xf * xf, axis=-1, keepdims=True)
    return xf * lax.rsqrt(ms + EPS) * g


def _silu(g):
    h = 0.5 * g
    return h + h * jnp.tanh(h)


def _const_spec(shape):
    return pl.BlockSpec(shape, lambda *_: (0,) * len(shape))


class _Layer:
    def __init__(self, stacked, l):
        self.stacked, self.l, self.shape = stacked, l, stacked.shape[1:]


def _weight_spec(w):
    if isinstance(w, _Layer):
        l, nd = w.l, len(w.shape)
        return pl.BlockSpec((None,) + w.shape, lambda *_: (l,) + (0,) * nd, pipeline_mode=pl.Buffered(1))
    return pl.BlockSpec(w.shape, lambda *_: (0,) * len(w.shape), pipeline_mode=pl.Buffered(1))


def _weight_arg(w):
    return w.stacked if isinstance(w, _Layer) else w


def _kv_down(xb, nkv_ref, wdc_ref, wdr_ref):
    hb = _rms(xb, nkv_ref[...]).astype(BF16)
    return _dot(hb, wdc_ref[...]), _dot(hb, wdr_ref[...])


def _kv_finish(lat, rr, cos, sin, nlat_ref):
    return _rms(lat, nlat_ref[...]), rr[:, :LANES] * cos + rr[:, LANES:] * sin


def _gla_kernel(*refs, rows, chunk, carry, n_prev, kv_blk):
    refs = list(refs)
    x_ref = refs.pop(0)
    s0_ref = None if carry else refs.pop(0)
    prev_ref = refs.pop(0) if n_prev else None
    npre_ref, win_ref, wglr_ref, wg2_ref, bg_ref, hn_ref, wout_ref, npost_ref = refs[:8]
    del refs[:8]
    if kv_blk:
        (cos_ref, sin_ref, koh_ref, nkv_ref, wdc_ref, wdr_ref, nlat_ref, wuk_ref, wuvt_ref) = refs[:9]
        del refs[:9]
    y_ref, sout_ref = refs[:2]
    del refs[:2]
    if kv_blk:
        ckv_ref, kr_ref, kn_ref, kr0_ref, vt_ref = refs[:5]
        del refs[:5]
    st_sc = refs.pop(0) if carry else None
    o_sc, qe_sc, u_sc, stb_sc = refs
    t = pl.program_id(1)

    if carry:
        @pl.when(t == 0)
        def _():
            st_sc[...] = jnp.zeros(st_sc.shape, F32)

    shift = chunk.bit_length() - 1
    sub = min(rows, GLA_SUB)
    blocks = range(0, rows, sub)
    xs, z, proj, b = {}, {}, {}, {}
    n_main = 2 * KD_A + 2 * VD_A
    for r in blocks:
        xs[r] = x_ref[0, r:r + sub, :]
        hb = _rms(xs[r], npre_ref[...]).astype(BF16)
        glr = _dot(hb, wglr_ref[...])[:, :GATE_RANK].astype(BF16)
        proj_qk = _dot(hb, win_ref[:, :2 * KD_A])
        z[r] = _dot(glr, wg2_ref[...]) + bg_ref[...]
        proj[r] = jnp.concatenate([proj_qk, _dot(hb, win_ref[:, 2 * KD_A:n_main])], axis=1)

    ri = lax.broadcasted_iota(jnp.int32, (sub, sub), 0)
    ci = lax.broadcasted_iota(jnp.int32, (sub, sub), 1)
    inchunk = jnp.where(jnp.right_shift(ri, shift) == jnp.right_shift(ci, shift), ci, sub) <= ri
    tri = jnp.where(inchunk, 1.0, 0.0).astype(BF16)
    for r in blocks:
        g = (jnp.minimum(z[r], 0.0) - jnp.log(1.0 + jnp.exp(-jnp.abs(z[r])))) * (LOG2E / GATE_TAU)
        g_hi = g.astype(BF16)
        g_lo = (g - g_hi.astype(F32)).astype(BF16)
        b[r] = _dot(tri, g_hi) + _dot(tri, g_lo)

    scale = DK_A ** -0.5
    nc = rows // chunk
    decay = [[None] * nc for _ in range(H_A)]

    for h in range(H_A):
        for r in blocks:
            chunks = range(r // chunk, (r + sub) // chunk)
            bh = b[r][:, h * DK_A:(h + 1) * DK_A]
            ends = {c: bh[(c + 1) * chunk - 1 - r:(c + 1) * chunk - r, :] for c in chunks}
            bl = jnp.concatenate([jnp.broadcast_to(ends[c], (chunk, DK_A)) for c in chunks], axis=0)
            qh = proj[r][:, h * DK_A:(h + 1) * DK_A]
            kh = proj[r][:, KD_A + h * DK_A:KD_A + (h + 1) * DK_A]
            vh = proj[r][:, 2 * KD_A + h * DV_A:2 * KD_A + (h + 1) * DV_A].astype(BF16)
            qe = (qh * scale * jnp.exp2(bh)).astype(BF16)
            ke = (kh * jnp.exp2(-bh)).astype(BF16)
            kd = (kh * jnp.exp2(bl - bh)).astype(BF16)
            s_raw = _dot_nt(qe, ke)
            qe_sc[r:r + sub, h * DK_A:(h + 1) * DK_A] = qe
            for c in chunks:
                r0, r1 = c * chunk - r, (c + 1) * chunk - r
                u_sc[h, c] = _dot_tn(kd[r0:r1], vh[r0:r1])
                dcol = jnp.broadcast_to(jnp.exp2(ends[c]), (LANES, DK_A)).T
                decay[h][c] = jnp.concatenate([dcol] * (DV_A // LANES), axis=1)
            s = jnp.where(inchunk, s_raw, 0.0).astype(BF16)
            o_sc[r:r + sub, h * DV_A:(h + 1) * DV_A] = _dot(s, vh)

    for h in range(H_A):
        if carry:
            st = st_sc[h]
        for c in range(nc):
            if not carry:
                st = s0_ref[c, h]
            stb_sc[h, c] = st.astype(BF16)
            st_new = st * decay[h][c] + u_sc[h, c]
            if carry:
                st = st_new
            else:
                sout_ref[c, h] = st_new
        if carry:
            st_sc[h] = st
        for c in range(nc):
            r0, r1 = c * chunk, (c + 1) * chunk
            o_sc[r0:r1, h * DV_A:(h + 1) * DV_A] += _dot(qe_sc[r0:r1, h * DK_A:(h + 1) * DK_A], stb_sc[h, c])

    x_new = {}
    for r in blocks:
        gate = proj[r][:, 2 * KD_A + VD_A:2 * KD_A + 2 * VD_A]
        heads = []
        for h in range(H_A):
            heads.append(_rms(o_sc[r:r + sub, h * DV_A:(h + 1) * DV_A], hn_ref[:, h * DV_A:(h + 1) * DV_A]))
        og = (jnp.concatenate(heads, axis=1) * _silu(gate)).astype(BF16)
        y = _dot(og, wout_ref[...])
        x_new[r] = xs[r] + _rms(y, npost_ref[...])
        y_ref[0, r:r + sub, :] = x_new[r]

    if kv_blk:
        down = {r: _kv_down(x_new[r], nkv_ref, wdc_ref, wdr_ref) for r in blocks}
        for r in blocks:
            ckv, kr2 = _kv_finish(*down[r], cos_ref[r:r + sub, :], sin_ref[r:r + sub, :], nlat_ref)
            ckv_ref[0, r:r + sub, :] = ckv
            kr_ref[0, r:r + sub, :] = kr2[:, :ROPE]
            cb = ckv.astype(BF16)
            kn_ref[0, r:r + sub, :] = _dot(cb, wuk_ref[...]).astype(BF16)
            lane = lax.broadcasted_iota(jnp.int32, kr2.shape, 1)
            kr0_ref[0, r:r + sub, :] = jnp.where(lane < ROPE, kr2, koh_ref[r:r + sub, :]).astype(BF16)
            vt = _dot_nt(wuvt_ref[...], cb).astype(BF16)
            for i in range(sub // kv_blk):
                vt_ref[0, r // kv_blk + i] = vt[:, i * kv_blk:(i + 1) * kv_blk]

    if carry:
        @pl.when(t == pl.num_programs(1) - 1)
        def _():
            for li in range(n_prev):
                sout_ref[li, 0] = prev_ref[li, 0]
            for h in range(H_A):
                sout_ref[n_prev, 0, h] = st_sc[h]


def _gla_scratch(rows, chunk):
    return [pltpu.VMEM((rows, VD_A), F32), pltpu.VMEM((rows, KD_A), BF16),
            pltpu.VMEM((H_A, rows // chunk, DK_A, DV_A), F32),
            pltpu.VMEM((H_A, rows // chunk, DK_A, DV_A), BF16)]


def _gla_prompt_layer(x, prev, weights, kv=None, *, rows, chunk, kv_blk=None):
    bsz, tlen, d = x.shape
    n_prev = 0 if prev is None else prev.shape[0]
    tok = lambda w: pl.BlockSpec((1, rows, w), lambda b, t: (b, t, 0))
    st_spec = lambda n: pl.BlockSpec((n, 1, H_A, DK_A, DV_A), lambda b, t: (0, b, 0, 0, 0))
    in_specs = [tok(d)] + ([st_spec(n_prev)] if n_prev else []) + [_weight_spec(a) for a in weights]
    args = [x] + ([prev] if n_prev else []) + [_weight_arg(a) for a in weights]
    out_specs = [tok(d), st_spec(n_prev + 1)]
    out_shape = [jax.ShapeDtypeStruct((bsz, tlen, d), F32),
                 jax.ShapeDtypeStruct((n_prev + 1, bsz, H_A, DK_A, DV_A), F32)]
    if kv is not None:
        assert min(rows, GLA_SUB) % kv_blk == 0
        tables, kv_weights = kv[:3], kv[3:]
        tab = pl.BlockSpec((rows, LANES), lambda b, t: (t, 0))
        in_specs += [tab] * 3 + [_weight_spec(a) for a in kv_weights]
        args += list(tables) + list(kv_weights)
        out_specs += [tok(KV_LORA), tok(ROPE), tok(H_B * NOPE), tok(LANES),
                      pl.BlockSpec((1, rows // kv_blk, VD_B, kv_blk), lambda b, t: (b, t, 0, 0))]
        out_shape += [jax.ShapeDtypeStruct((bsz, tlen, KV_LORA), F32),
                      jax.ShapeDtypeStruct((bsz, tlen, ROPE), F32),
                      jax.ShapeDtypeStruct((bsz, tlen, H_B * NOPE), BF16),
                      jax.ShapeDtypeStruct((bsz, tlen, LANES), BF16),
                      jax.ShapeDtypeStruct((bsz, tlen // kv_blk, VD_B, kv_blk), BF16)]
    return pl.pallas_call(
        functools.partial(_gla_kernel, rows=rows, chunk=chunk, carry=True, n_prev=n_prev,
                          kv_blk=kv_blk if kv is not None else None),
        grid=(bsz, tlen // rows),
        in_specs=in_specs,
        out_specs=tuple(out_specs),
        out_shape=tuple(out_shape),
        scratch_shapes=[pltpu.VMEM((H_A, DK_A, DV_A), F32)] + _gla_scratch(rows, chunk),
        compiler_params=pltpu.CompilerParams(
            dimension_semantics=("parallel", "arbitrary"), vmem_limit_bytes=VMEM_LIMIT),
        name="gla_layer",
    )(*args)


def _gla_sample_layer(x, states, layer, weights, *, groups):
    bsz, t, d = x.shape
    per = bsz // groups
    rows = per * t
    tok = pl.BlockSpec((1, rows, d), lambda g, _: (g, 0, 0))
    st_spec = pl.BlockSpec((per, H_A, DK_A, DV_A), lambda g, _: (g, 0, 0, 0))
    st_in = pl.BlockSpec((None, per, H_A, DK_A, DV_A), lambda g, _: (layer, g, 0, 0, 0))
    y, s_new = pl.pallas_call(
        functools.partial(_gla_kernel, rows=rows, chunk=t, carry=False, n_prev=0, kv_blk=None),
        grid=(groups, 1),
        in_specs=[tok, st_in] + [_weight_spec(a) for a in weights],
        out_specs=(tok, st_spec),
        out_shape=(jax.ShapeDtypeStruct((groups, rows, d), F32),
                   jax.ShapeDtypeStruct((bsz, H_A, DK_A, DV_A), F32)),
        scratch_shapes=_gla_scratch(rows, t),
        compiler_params=pltpu.CompilerParams(
            dimension_semantics=("parallel", "arbitrary"), vmem_limit_bytes=VMEM_LIMIT),
        name="gla_sample_layer",
    )(x.reshape(groups, rows, d), states, *[_weight_arg(a) for a in weights])
    return y.reshape(bsz, t, d), s_new


def _kv_kernel(x_ref, cos_ref, sin_ref, nkv_ref, wdc_ref, wdr_ref, nlat_ref, ckv_ref, kr_ref):
    lat, rr = _kv_down(x_ref[...], nkv_ref, wdc_ref, wdr_ref)
    ckv, kr2 = _kv_finish(lat, rr, cos_ref[...], sin_ref[...], nlat_ref)
    ckv_ref[...] = ckv
    kr_ref[...] = kr2[:, :ROPE]


def _shared_kv_latent(x, cos2, sin2, nkv, wdc, wdr, nlat):
    n, d = x.shape
    args = [x, cos2, sin2, nkv, wdc, wdr, nlat]
    return pl.pallas_call(
        _kv_kernel,
        grid=(1,),
        in_specs=[_const_spec(a.shape) for a in args],
        out_specs=(_const_spec((n, KV_LORA)), _const_spec((n, ROPE))),
        out_shape=(jax.ShapeDtypeStruct((n, KV_LORA), F32), jax.ShapeDtypeStruct((n, ROPE), F32)),
        compiler_params=pltpu.CompilerParams(
            dimension_semantics=("arbitrary",), vmem_limit_bytes=VMEM_LIMIT),
        name="shared_kv",
    )(*args)


def _mla_q(x, cos, sin, npre_ref, win_ref, qn_ref, wq_ref, rope_w):
    hb = _rms(x, npre_ref[...]).astype(BF16)
    proj = _dot(hb, win_ref[...])
    qn = _rms(proj[:, :Q_LORA], qn_ref[...]).astype(BF16)
    qq = _dot(qn, wq_ref[...])
    scale = (NOPE + ROPE) ** -0.5
    n0, n1 = H_B * NOPE, H_B * NOPE + H_B * rope_w
    q_nope = (qq[:, :n0] * scale).astype(BF16)
    q_rope = ((qq[:, n0:n1] * cos + qq[:, n1:] * sin) * scale).astype(BF16)
    return q_nope, q_rope, proj[:, Q_LORA:]


ACC_ROWS = V_B + 16


def _mla_kernel(x_ref, kn_ref, kr_ref, vt_ref, cos_ref, sin_ref, npre_ref, win_ref, qn_ref, wqt_ref,
                wout_ref, npost_ref, y_ref, q_sc, m_sc, acc_sc, alpha_sc, p_sc, *, tq, tk):
    i = pl.program_id(1)
    c = (NOPE + ROPE) ** -0.5 * LOG2E
    n0, n1 = H_B * NOPE, H_B * (NOPE + ROPE)
    x = x_ref[0]
    hbs = {r: _rms(x[r:r + tk], npre_ref[...]).astype(BF16) for r in range(0, tq, tk)}
    qlat = jnp.concatenate([_dot(hbs[r], win_ref[:, :Q_LORA]) for r in hbs], axis=0)
    gates = {r: _dot(hbs[r], win_ref[:, Q_LORA:]) for r in hbs}
    qnt = _rms(qlat, qn_ref[...]).T.astype(BF16)
    qqt = _dot(wqt_ref[...], qnt)
    q_sc[:n0, :] = (qqt[:n0] * c).astype(BF16)
    cos, sin = cos_ref[...], sin_ref[...]
    for h in range(H_B):
        r0 = n0 + h * ROPE
        x1, x2 = qqt[r0:r0 + HALF], qqt[r0 + HALF:r0 + ROPE]
        q_sc[r0:r0 + HALF, :] = ((x1 * cos - x2 * sin) * c).astype(BF16)
        q_sc[r0 + HALF:r0 + ROPE, :] = ((x2 * cos + x1 * sin) * c).astype(BF16)
    row = lax.broadcasted_iota(jnp.int32, (ROPE, tq), 0)
    q_chunk = jnp.right_shift(i * tq + lax.broadcasted_iota(jnp.int32, (ROPE, tq), 1), CHUNK.bit_length() - 1)
    q_sc[n1:, :] = jnp.where(row > q_chunk, MASKED, 0.0).astype(BF16)
    m_sc[...] = jnp.full(m_sc.shape, -jnp.inf, F32)
    acc_sc[...] = jnp.zeros(acc_sc.shape, F32)

    ones = jnp.ones((ACC_ROWS - V_B, tk), BF16)

    def scores(j, h, lo, hi):
        k0 = pl.multiple_of(j * tk, tk)
        kb = jnp.concatenate(
            [kn_ref[0, pl.ds(k0, tk), h * NOPE:(h + 1) * NOPE], kr_ref[0, pl.ds(k0, tk), :]], axis=1)
        qt = jnp.concatenate(
            [q_sc[h * NOPE:(h + 1) * NOPE, lo:hi], q_sc[n0 + h * ROPE:n0 + (h + 1) * ROPE, lo:hi],
             q_sc[n1:, lo:hi]], axis=0)
        return _dot(kb, qt)

    def softmax(s, h, slot, lo, hi):
        m_prev = m_sc[h, :, lo:hi]
        m_new = jnp.maximum(m_prev, jnp.max(s, axis=0, keepdims=True))
        alpha_sc[slot, h, :, lo:hi] = jnp.exp2(m_prev - m_new)
        p_sc[slot, h, :, lo:hi] = jnp.exp2(s - m_new).astype(BF16)
        m_sc[h, :, lo:hi] = m_new

    def values(j, h, slot, lo, hi):
        vt = jnp.concatenate([vt_ref[0, j, h * V_B:(h + 1) * V_B, :], ones], axis=0)
        acc_sc[h, :, lo:hi] = (alpha_sc[slot, h, :, lo:hi] * acc_sc[h, :, lo:hi]
                               + _dot(vt, p_sc[slot, h, :, lo:hi]))

    for h in range(H_B):
        softmax(scores(0, h, 0, tq), h, 0, 0, tq)

    def block(j, slot_new, slot_old, lo=0):
        s = [scores(j, h, lo, tq) for h in range(H_B)]
        for h in range(H_B):
            values(j - 1, h, slot_old, 0, tq)
        for h in range(H_B):
            softmax(s[h], h, slot_new, lo, tq)

    def pair(jj, carry):
        block(2 * jj + 1, 1, 0)
        block(2 * jj + 2, 0, 1)
        return carry

    def tail_odd(last, lo=0):
        block(last, 1, 0, lo)
        for h in range(H_B):
            values(last, h, 1, lo, tq)

    def tail_even(last):
        for h in range(H_B):
            values(last, h, 0, 0, tq)

    if tq == 2 * tk:
        lax.fori_loop(0, i, pair, 0)
        tail_odd(2 * i + 1, tk)
    else:
        lax.fori_loop(0, i // 2, pair, 0)
        odd = lax.rem(i, 2) == 1
        pl.when(odd)(lambda: tail_odd(i))
        pl.when(jnp.logical_not(odd))(lambda: tail_even(i))

    for r in range(0, tq, tk):
        gate = gates[r]
        outs = []
        for h in range(H_B):
            a = acc_sc[h, :, r:r + tk]
            outs.append(a[:V_B] * (1.0 / a[V_B:V_B + 1]))
        o = jnp.concatenate(outs, axis=0).T
        og = (o * _silu(gate)).astype(BF16)
        y = _dot(og, wout_ref[...])
        y_ref[0, r:r + tk, :] = x[r:r + tk] + _rms(y, npost_ref[...])


def _mla_prompt_layer(x, kn, kr0, vt, cos8t, sin8t, npre, win, qn, wqt, wout, npost, *, tq, tk):
    bsz, tlen, d = x.shape
    assert tq in (tk, 2 * tk) and tk % CHUNK == 0 and vt.shape[3] == tk
    tok = pl.BlockSpec((1, tq, d), lambda b, t: (b, t, 0))
    full = lambda w: pl.BlockSpec((1, tlen, w), lambda b, t: (b, 0, 0))
    tab = pl.BlockSpec((cos8t.shape[0], tq), lambda b, t: (0, t))
    weights = [npre, win, qn, wqt, wout, npost]
    return pl.pallas_call(
        functools.partial(_mla_kernel, tq=tq, tk=tk),
        grid=(bsz, tlen // tq),
        in_specs=[tok, full(kn.shape[2]), full(kr0.shape[2]),
                  pl.BlockSpec((1,) + vt.shape[1:], lambda b, t: (b, 0, 0, 0)), tab, tab]
                 + [_weight_spec(a) for a in weights],
        out_specs=tok,
        out_shape=jax.ShapeDtypeStruct((bsz, tlen, d), F32),
        scratch_shapes=[pltpu.VMEM((H_B * (NOPE + ROPE) + ROPE, tq), BF16),
                        pltpu.VMEM((H_B, 1, tq), F32),
                        pltpu.VMEM((H_B, ACC_ROWS, tq), F32),
                        pltpu.VMEM((2, H_B, 1, tq), F32),
                        pltpu.VMEM((2, H_B, tk, tq), BF16)],
        compiler_params=pltpu.CompilerParams(
            dimension_semantics=("parallel", "arbitrary"), vmem_limit_bytes=VMEM_LIMIT),
        name="mla_prompt_layer",
    )(x, kn, kr0, vt, cos8t, sin8t, *[_weight_arg(a) for a in weights])


def _mla_sample_q_kernel(x_ref, cos_ref, sin_ref, npre_ref, win_ref, qn_ref, wq_ref, wuk_ref,
                         qlat_ref, qr_ref, gate_ref):
    q_nope, q_rope, gate = _mla_q(x_ref[...], cos_ref[...], sin_ref[...], npre_ref, win_ref, qn_ref,
                                  wq_ref, LANES)
    for h in range(H_B):
        qlat_ref[:, h * KV_LORA:(h + 1) * KV_LORA] = _dot_nt(
            q_nope[:, h * NOPE:(h + 1) * NOPE], wuk_ref[:, h * NOPE:(h + 1) * NOPE]).astype(BF16)
    qr_ref[...] = q_rope
    gate_ref[...] = gate


def _mla_sample_out_kernel(x_ref, olat_ref, gate_ref, wuv_ref, wout_ref, npost_ref, y_ref):
    o = jnp.concatenate(
        [_dot(olat_ref[:, h * KV_LORA:(h + 1) * KV_LORA], wuv_ref[:, h * V_B:(h + 1) * V_B])
         for h in range(H_B)], axis=1)
    og = (o * _silu(gate_ref[...])).astype(BF16)
    y = _dot(og, wout_ref[...])
    y_ref[...] = x_ref[...] + _rms(y, npost_ref[...])


def _mla_sample_attn_kernel(qlat_ref, qr_ref, cckv_ref, ckr_ref, nckv_ref, nkr_ref, olat_ref, *, past):
    nseq, t = qlat_ref.shape[0], qlat_ref.shape[1]
    cshift = CHUNK.bit_length() - 1

    def chunk_mask(s, k_start):
        rows = lax.broadcasted_iota(jnp.int32, (s.shape[0], 1), 0)
        cols = lax.broadcasted_iota(jnp.int32, (1, s.shape[1]), 1)
        q_chunk = jnp.right_shift(past + jnp.bitwise_and(rows, t - 1), cshift)
        k_chunk = jnp.right_shift(k_start + cols, cshift)
        return jnp.where(k_chunk <= q_chunk, s, -jnp.inf)

    c_ckv, n_ckv, s_c, s_n = {}, {}, {}, {}
    for i in range(nseq):
        qlat = jnp.concatenate([qlat_ref[i, :, h * KV_LORA:(h + 1) * KV_LORA] for h in range(H_B)], axis=0)
        qr = jnp.concatenate([qr_ref[i, :, h * LANES:h * LANES + ROPE] for h in range(H_B)], axis=0)
        c_ckv[i] = cckv_ref[i].astype(BF16)
        n_ckv[i] = nckv_ref[i].astype(BF16)
        s_c[i] = _dot_nt(qlat, c_ckv[i]) + _dot_nt(qr, ckr_ref[i].astype(BF16))
        s_n[i] = _dot_nt(qlat, n_ckv[i]) + _dot_nt(qr, nkr_ref[i].astype(BF16))
    for i in range(nseq):
        sc, sn = chunk_mask(s_c[i], 0), chunk_mask(s_n[i], past)
        m = jnp.maximum(jnp.max(sc, axis=-1, keepdims=True), jnp.max(sn, axis=-1, keepdims=True))
        p_c = jnp.exp(sc - m)
        p_n = jnp.exp(sn - m)
        l = jnp.sum(p_c, axis=-1, keepdims=True) + jnp.sum(p_n, axis=-1, keepdims=True)
        olat = ((_dot(p_c.astype(BF16), c_ckv[i]) + _dot(p_n.astype(BF16), n_ckv[i])) * (1.0 / l)).astype(BF16)
        for h in range(H_B):
            olat_ref[i, :, h * KV_LORA:(h + 1) * KV_LORA] = olat[h * t:(h + 1) * t]


def _one_step_call(body, args, out_shapes, name):
    return pl.pallas_call(
        body,
        grid=(1,),
        in_specs=[_weight_spec(a) for a in args],
        out_specs=tuple(_const_spec(s.shape) for s in out_shapes),
        out_shape=tuple(out_shapes),
        compiler_params=pltpu.CompilerParams(
            dimension_semantics=("arbitrary",), vmem_limit_bytes=VMEM_LIMIT),
        name=name,
    )(*[_weight_arg(a) for a in args])


def _mla_sample_layer(x, cache_ckv, cache_kr, new_ckv, new_kr, cos, sin, npre, win, qn, wq, wuk, wuv,
                      wout, npost):
    bsz, t, d = x.shape
    past = cache_ckv.shape[1]
    assert t & (t - 1) == 0
    n = bsz * t
    x2 = x.reshape(n, d)
    qlat, qr, gate = _one_step_call(
        _mla_sample_q_kernel, [x2, cos, sin, npre, win, qn, wq, wuk],
        [jax.ShapeDtypeStruct((n, H_B * KV_LORA), BF16), jax.ShapeDtypeStruct((n, H_B * LANES), BF16),
         jax.ShapeDtypeStruct((n, VD_B), F32)], "mla_sample_q")
    nseq = SAMPLE_ATTN_SEQS if bsz % SAMPLE_ATTN_SEQS == 0 else 1
    per_b = lambda a: pl.BlockSpec((nseq,) + a.shape[1:], lambda b: (b, 0, 0))
    attn_args = [qlat.reshape(bsz, t, -1), qr.reshape(bsz, t, -1), cache_ckv, cache_kr, new_ckv, new_kr]
    olat = pl.pallas_call(
        functools.partial(_mla_sample_attn_kernel, past=past),
        grid=(bsz // nseq,),
        in_specs=[per_b(a) for a in attn_args],
        out_specs=pl.BlockSpec((nseq, t, H_B * KV_LORA), lambda b: (b, 0, 0)),
        out_shape=jax.ShapeDtypeStruct((bsz, t, H_B * KV_LORA), BF16),
        compiler_params=pltpu.CompilerParams(
            dimension_semantics=("parallel",), vmem_limit_bytes=VMEM_LIMIT),
        name="mla_sample_attn",
    )(*attn_args)
    (y,) = _one_step_call(
        _mla_sample_out_kernel, [x2, olat.reshape(n, -1), gate, wuv, wout, npost],
        [jax.ShapeDtypeStruct((n, d), F32)], "mla_sample_out")
    return y.reshape(bsz, t, d)


def _rope_tables(pos_offset, tlen):
    inv = jnp.power(ROPE_BASE, -jnp.arange(HALF, dtype=F32) / HALF)
    ang = (pos_offset + jnp.arange(tlen)).astype(F32)[:, None] * inv[None, :]
    cos = jnp.cos(ang)
    sin = jnp.sin(ang)
    return jnp.concatenate([cos, cos], axis=1), jnp.concatenate([sin, sin], axis=1)


def _rot_cols(w):
    return jnp.concatenate([-w[..., HALF:], w[..., :HALF]], axis=-1)


def _row(a):
    return a.reshape(1, -1)


def kernel(x_prompt, x_sample, state_gla, cache_ckv, cache_krope, a_norm_pre, a_w_in, a_w_gate2, a_b_gate,
           a_head_norm, a_w_out, a_norm_post, kv_norm, kv_w_down, kv_latent_norm, kv_w_uk, kv_w_uv,
           b_norm_pre, b_w_in, b_q_norm, b_w_uq, b_w_out, b_norm_post):
    past = cache_ckv.shape[1]
    t_p, t_s = x_prompt.shape[1], x_sample.shape[1]

    a_win = a_w_in.astype(BF16)
    a_wglr = jnp.pad(a_w_in[:, :, 2 * KD_A + 2 * VD_A:],
                     ((0, 0), (0, 0), (0, 2 * LANES - GATE_RANK))).astype(BF16)
    a_wg2 = a_w_gate2.astype(BF16)
    a_wout = a_w_out.astype(BF16)
    wdc = kv_w_down[:, :KV_LORA].astype(BF16)
    wdr_cols = kv_w_down[:, KV_LORA:]
    wdr = jnp.concatenate([wdr_cols, wdr_cols, _rot_cols(wdr_cols), _rot_cols(wdr_cols)], axis=1).astype(BF16)
    wuk = kv_w_uk.astype(BF16)
    wuv = kv_w_uv.astype(BF16)
    b_win = b_w_in.astype(BF16)
    b_wout = b_w_out.astype(BF16)
    uq = b_w_uq.reshape(N_B, Q_LORA, H_B, NOPE + ROPE)
    uq_nope = uq[..., :NOPE].reshape(N_B, Q_LORA, H_B * NOPE)
    uq_rope = uq[..., NOPE:]
    uq_rot = _rot_cols(uq_rope)
    wqt_p = jnp.concatenate([uq_nope, uq_rope.reshape(N_B, Q_LORA, H_B * ROPE)],
                            axis=-1).astype(BF16).transpose(0, 2, 1)
    wuvt = wuv.T
    pad = lambda w: jnp.pad(w, ((0, 0), (0, 0), (0, 0), (0, LANES - ROPE))).reshape(N_B, Q_LORA, H_B * LANES)
    wq_s = jnp.concatenate([uq_nope, pad(uq_rope), pad(uq_rot)], axis=-1).astype(BF16)

    cos_p, sin_p = _rope_tables(0, t_p)
    cos_s, sin_s = _rope_tables(past, t_s)
    tile2 = lambda a: jnp.concatenate([a, a], axis=1)
    cos8t_p, sin8t_p = cos_p[:, :HALF].T, sin_p[:, :HALF].T
    padl = lambda a: jnp.tile(jnp.pad(a, ((0, 0), (0, LANES - ROPE))), (1, H_B))
    cos8_s, sin8_s = padl(cos_s), padl(sin_s)

    def lay(a, l):
        return _Layer(a.reshape(a.shape[0], 1, -1) if a.ndim == 2 else a, l)

    def gla_weights(l):
        return (lay(a_norm_pre, l), lay(a_win, l), lay(a_wglr, l), lay(a_wg2, l), lay(a_b_gate, l),
                lay(a_head_norm, l), lay(a_wout, l), lay(a_norm_post, l))

    assert t_p // CHUNK <= LANES - ROPE
    koh = jax.nn.one_hot(ROPE + jnp.arange(t_p) // CHUNK, LANES, dtype=F32)
    kv_args = (tile2(cos_p), tile2(sin_p), koh, _row(kv_norm), wdc, wdr, _row(kv_latent_norm), wuk, wuvt)
    x, st_p = x_prompt, None
    for l in range(N_A - 1):
        x, st_p = _gla_prompt_layer(x, st_p, gla_weights(l), rows=GLA_ROWS, chunk=CHUNK)
    x, st_p, ckv_p, kr_p, kn, kr0, vt = _gla_prompt_layer(
        x, st_p, gla_weights(N_A - 1), kv_args, rows=GLA_ROWS, chunk=CHUNK, kv_blk=ATT_K)
    for j in range(N_B):
        x = _mla_prompt_layer(x, kn, kr0, vt, cos8t_p, sin8t_p, lay(b_norm_pre, j), lay(b_win, j),
                              lay(b_q_norm, j), lay(wqt_p, j), lay(b_wout, j), lay(b_norm_post, j),
                              tq=ATT_Q, tk=ATT_K)
    y_prompt = x

    assert t_s <= CHUNK
    bsz_s = x_sample.shape[0]
    x, states = x_sample, []
    for l in range(N_A):
        x, s_new = _gla_sample_layer(x, state_gla, l, gla_weights(l), groups=SAMPLE_GROUPS)
        states.append(s_new)
    st_s = jnp.stack(states)
    per_row = lambda a: jnp.tile(tile2(a), (bsz_s, 1))
    ckv_s, kr_s = _shared_kv_latent(x.reshape(bsz_s * t_s, D_MODEL), per_row(cos_s), per_row(sin_s),
                                    _row(kv_norm), wdc, wdr, _row(kv_latent_norm))
    ckv_s = ckv_s.reshape(bsz_s, t_s, KV_LORA)
    kr_s = kr_s.reshape(bsz_s, t_s, ROPE)
    cos_rows, sin_rows = jnp.tile(cos8_s, (bsz_s, 1)), jnp.tile(sin8_s, (bsz_s, 1))
    for j in range(N_B):
        x = _mla_sample_layer(x, cache_ckv, cache_krope, ckv_s, kr_s, cos_rows, sin_rows,
                              lay(b_norm_pre, j), lay(b_win, j), lay(b_q_norm, j), lay(wq_s, j), wuk, wuv,
                              lay(b_wout, j), lay(b_norm_post, j))
    y_sample = x

    return (y_prompt, y_sample, st_p, st_s, ckv_p, kr_p, ckv_s, kr_s)
```

```python
import functools

import jax
import jax.numpy as jnp
from jax import lax
from jax.experimental import pallas as pl
from jax.experimental.pallas import tpu as pltpu

F32 = jnp.float32
BF16 = jnp.bfloat16

D_MODEL = 1024
CHUNK = 64
N_A = 2
N_B = 2
H_A = 4
DK_A = 128
DV_A = 256
KD_A = H_A * DK_A
VD_A = H_A * DV_A
GATE_RANK = 16
GATE_TAU = 16.0
H_B = 8
NOPE = 128
ROPE = 64
HALF = ROPE // 2
V_B = 128
VD_B = H_B * V_B
Q_LORA = 384
KV_LORA = 256
ROPE_BASE = 10000.0
EPS = 1e-6

LANES = 128
VMEM_LIMIT = 56 * 1024 * 1024

GLA_ROWS = 512
GLA_SUB = 256
SAMPLE_ATTN_SEQS = 4
SAMPLE_GROUPS = 2
ATT_Q = 512
ATT_K = 256
LOG2E = 1.4426950408889634
MASKED = -1e30


def _dot(a, b):
    return jnp.dot(a, b, preferred_element_type=F32)


def _dot_nt(a, b):
    return lax.dot_general(a, b, (((1,), (1,)), ((), ())), preferred_element_type=F32)


def _dot_tn(a, b):
    return lax.dot_general(a, b, (((0,), (0,)), ((), ())), preferred_element_type=F32)


def _rms(xf, g):
    ms = jnp.mean(xf * xf, axis=-1, keepdims=True)
    return xf * lax.rsqrt(ms + EPS) * g


def _silu(g):
    h = 0.5 * g
    return h + h * jnp.tanh(h)


def _const_spec(shape):
    return pl.BlockSpec(shape, lambda *_: (0,) * len(shape))


class _Layer:
    def __init__(self, stacked, l):
        self.stacked, self.l, self.shape = stacked, l, stacked.shape[1:]


def _weight_spec(w):
    if isinstance(w, _Layer):
        l, nd = w.l, len(w.shape)
        return pl.BlockSpec((None,) + w.shape, lambda *_: (l,) + (0,) * nd, pipeline_mode=pl.Buffered(1))
    return pl.BlockSpec(w.shape, lambda *_: (0,) * len(w.shape), pipeline_mode=pl.Buffered(1))


def _weight_arg(w):
    return w.stacked if isinstance(w, _Layer) else w


def _kv_down(xb, nkv_ref, wdc_ref, wdr_ref):
    hb = _rms(xb, nkv_ref[...]).astype(BF16)
    return _dot(hb, wdc_ref[...]), _dot(hb, wdr_ref[...])


def _kv_finish(lat, rr, cos, sin, nlat_ref):
    return _rms(lat, nlat_ref[...]), rr[:, :LANES] * cos + rr[:, LANES:] * sin


def _gla_kernel(*refs, rows, chunk, carry, n_prev, kv_blk):
    refs = list(refs)
    x_ref = refs.pop(0)
    s0_ref = None if carry else refs.pop(0)
    prev_ref = refs.pop(0) if n_prev else None
    npre_ref, win_ref, wglr_ref, wg2_ref, bg_ref, hn_ref, wout_ref, npost_ref = refs[:8]
    del refs[:8]
    if kv_blk:
        (cos_ref, sin_ref, koh_ref, nkv_ref, wdc_ref, wdr_ref, nlat_ref, wuk_ref, wuvt_ref) = refs[:9]
        del refs[:9]
    y_ref, sout_ref = refs[:2]
    del refs[:2]
    if kv_blk:
        ckv_ref, kr_ref, kn_ref, kr0_ref, vt_ref = refs[:5]
        del refs[:5]
    st_sc = refs.pop(0) if carry else None
    o_sc, qe_sc, u_sc, stb_sc = refs
    t = pl.program_id(1)

    if carry:
        @pl.when(t == 0)
        def _():
            st_sc[...] = jnp.zeros(st_sc.shape, F32)

    shift = chunk.bit_length() - 1
    sub = min(rows, GLA_SUB)
    blocks = range(0, rows, sub)
    xs, z, proj, b = {}, {}, {}, {}
    n_main = 2 * KD_A + 2 * VD_A
    for r in blocks:
        xs[r] = x_ref[0, r:r + sub, :]
        hb = _rms(xs[r], npre_ref[...]).astype(BF16)
        glr = _dot(hb, wglr_ref[...])[:, :GATE_RANK].astype(BF16)
        proj_qk = _dot(hb, win_ref[:, :2 * KD_A])
        z[r] = _dot(glr, wg2_ref[...]) + bg_ref[...]
        proj[r] = jnp.concatenate([proj_qk, _dot(hb, win_ref[:, 2 * KD_A:n_main])], axis=1)

    ri = lax.broadcasted_iota(jnp.int32, (sub, sub), 0)
    ci = lax.broadcasted_iota(jnp.int32, (sub, sub), 1)
    inchunk = jnp.where(jnp.right_shift(ri, shift) == jnp.right_shift(ci, shift), ci, sub) <= ri
    tri = jnp.where(inchunk, 1.0, 0.0).astype(BF16)
    for r in blocks:
        g = (jnp.minimum(z[r], 0.0) - jnp.log(1.0 + jnp.exp(-jnp.abs(z[r])))) * (LOG2E / GATE_TAU)
        g_hi = g.astype(BF16)
        g_lo = (g - g_hi.astype(F32)).astype(BF16)
        b[r] = _dot(tri, g_hi) + _dot(tri, g_lo)

    scale = DK_A ** -0.5
    nc = rows // chunk
    decay = [[None] * nc for _ in range(H_A)]

    for h in range(H_A):
        for r in blocks:
            chunks = range(r // chunk, (r + sub) // chunk)
            bh = b[r][:, h * DK_A:(h + 1) * DK_A]
            ends = {c: bh[(c + 1) * chunk - 1 - r:(c + 1) * chunk - r, :] for c in chunks}
            bl = jnp.concatenate([jnp.broadcast_to(ends[c], (chunk, DK_A)) for c in chunks], axis=0)
            qh = proj[r][:, h * DK_A:(h + 1) * DK_A]
            kh = proj[r][:, KD_A + h * DK_A:KD_A + (h + 1) * DK_A]
            vh = proj[r][:, 2 * KD_A + h * DV_A:2 * KD_A + (h + 1) * DV_A].astype(BF16)
            qe = (qh * scale * jnp.exp2(bh)).astype(BF16)
            ke = (kh * jnp.exp2(-bh)).astype(BF16)
            kd = (kh * jnp.exp2(bl - bh)).astype(BF16)
            s_raw = _dot_nt(qe, ke)
            qe_sc[r:r + sub, h * DK_A:(h + 1) * DK_A] = qe
            for c in chunks:
                r0, r1 = c * chunk - r, (c + 1) * chunk - r
                u_sc[h, c] = _dot_tn(kd[r0:r1], vh[r0:r1])
                dcol = jnp.broadcast_to(jnp.exp2(ends[c]), (LANES, DK_A)).T
                decay[h][c] = jnp.concatenate([dcol] * (DV_A // LANES), axis=1)
            s = jnp.where(inchunk, s_raw, 0.0).astype(BF16)
            o_sc[r:r + sub, h * DV_A:(h + 1) * DV_A] = _dot(s, vh)

    for h in range(H_A):
        if carry:
            st = st_sc[h]
        for c in range(nc):
            if not carry:
                st = s0_ref[c, h]
            stb_sc[h, c] = st.astype(BF16)
            st_new = st * decay[h][c] + u_sc[h, c]
            if carry:
                st = st_new
            else:
                sout_ref[c, h] = st_new
        if carry:
            st_sc[h] = st
        for c in range(nc):
            r0, r1 = c * chunk, (c + 1) * chunk
            o_sc[r0:r1, h * DV_A:(h + 1) * DV_A] += _dot(qe_sc[r0:r1, h * DK_A:(h + 1) * DK_A], stb_sc[h, c])

    x_new = {}
    for r in blocks:
        gate = proj[r][:, 2 * KD_A + VD_A:2 * KD_A + 2 * VD_A]
        heads = []
        for h in range(H_A):
            heads.append(_rms(o_sc[r:r + sub, h * DV_A:(h + 1) * DV_A], hn_ref[:, h * DV_A:(h + 1) * DV_A]))
        og = (jnp.concatenate(heads, axis=1) * _silu(gate)).astype(BF16)
        y = _dot(og, wout_ref[...])
        x_new[r] = xs[r] + _rms(y, npost_ref[...])
        y_ref[0, r:r + sub, :] = x_new[r]

    if kv_blk:
        down = {r: _kv_down(x_new[r], nkv_ref, wdc_ref, wdr_ref) for r in blocks}
        for r in blocks:
            ckv, kr2 = _kv_finish(*down[r], cos_ref[r:r + sub, :], sin_ref[r:r + sub, :], nlat_ref)
            ckv_ref[0, r:r + sub, :] = ckv
            kr_ref[0, r:r + sub, :] = kr2[:, :ROPE]
            cb = ckv.astype(BF16)
            kn_ref[0, r:r + sub, :] = _dot(cb, wuk_ref[...]).astype(BF16)
            lane = lax.broadcasted_iota(jnp.int32, kr2.shape, 1)
            kr0_ref[0, r:r + sub, :] = jnp.where(lane < ROPE, kr2, koh_ref[r:r + sub, :]).astype(BF16)
            vt = _dot_nt(wuvt_ref[...], cb).astype(BF16)
            for i in range(sub // kv_blk):
                vt_ref[0, r // kv_blk + i] = vt[:, i * kv_blk:(i + 1) * kv_blk]

    if carry:
        @pl.when(t == pl.num_programs(1) - 1)
        def _():
            for li in range(n_prev):
                sout_ref[li, 0] = prev_ref[li, 0]
            for h in range(H_A):
                sout_ref[n_prev, 0, h] = st_sc[h]


def _gla_scratch(rows, chunk):
    return [pltpu.VMEM((rows, VD_A), F32), pltpu.VMEM((rows, KD_A), BF16),
            pltpu.VMEM((H_A, rows // chunk, DK_A, DV_A), F32),
            pltpu.VMEM((H_A, rows // chunk, DK_A, DV_A), BF16)]


def _gla_prompt_layer(x, prev, weights, kv=None, *, rows, chunk, kv_blk=None):
    bsz, tlen, d = x.shape
    n_prev = 0 if prev is None else prev.shape[0]
    tok = lambda w: pl.BlockSpec((1, rows, w), lambda b, t: (b, t, 0))
    st_spec = lambda n: pl.BlockSpec((n, 1, H_A, DK_A, DV_A), lambda b, t: (0, b, 0, 0, 0))
    in_specs = [tok(d)] + ([st_spec(n_prev)] if n_prev else []) + [_weight_spec(a) for a in weights]
    args = [x] + ([prev] if n_prev else []) + [_weight_arg(a) for a in weights]
    out_specs = [tok(d), st_spec(n_prev + 1)]
    out_shape = [jax.ShapeDtypeStruct((bsz, tlen, d), F32),
                 jax.ShapeDtypeStruct((n_prev + 1, bsz, H_A, DK_A, DV_A), F32)]
    if kv is not None:
        assert min(rows, GLA_SUB) % kv_blk == 0
        tables, kv_weights = kv[:3], kv[3:]
        tab = pl.BlockSpec((rows, LANES), lambda b, t: (t, 0))
        in_specs += [tab] * 3 + [_weight_spec(a) for a in kv_weights]
        args += list(tables) + list(kv_weights)
        out_specs += [tok(KV_LORA), tok(ROPE), tok(H_B * NOPE), tok(LANES),
                      pl.BlockSpec((1, rows // kv_blk, VD_B, kv_blk), lambda b, t: (b, t, 0, 0))]
        out_shape += [jax.ShapeDtypeStruct((bsz, tlen, KV_LORA), F32),
                      jax.ShapeDtypeStruct((bsz, tlen, ROPE), F32),
                      jax.ShapeDtypeStruct((bsz, tlen, H_B * NOPE), BF16),
                      jax.ShapeDtypeStruct((bsz, tlen, LANES), BF16),
                      jax.ShapeDtypeStruct((bsz, tlen // kv_blk, VD_B, kv_blk), BF16)]
    return pl.pallas_call(
        functools.partial(_gla_kernel, rows=rows, chunk=chunk, carry=True, n_prev=n_prev,
                          kv_blk=kv_blk if kv is not None else None),
        grid=(bsz, tlen // rows),
        in_specs=in_specs,
        out_specs=tuple(out_specs),
        out_shape=tuple(out_shape),
        scratch_shapes=[pltpu.VMEM((H_A, DK_A, DV_A), F32)] + _gla_scratch(rows, chunk),
        compiler_params=pltpu.CompilerParams(
            dimension_semantics=("parallel", "arbitrary"), vmem_limit_bytes=VMEM_LIMIT),
        name="gla_layer",
    )(*args)


def _gla_sample_layer(x, states, layer, weights, *, groups):
    bsz, t, d = x.shape
    per = bsz // groups
    rows = per * t
    tok = pl.BlockSpec((1, rows, d), lambda g, _: (g, 0, 0))
    st_spec = pl.BlockSpec((per, H_A, DK_A, DV_A), lambda g, _: (g, 0, 0, 0))
    st_in = pl.BlockSpec((None, per, H_A, DK_A, DV_A), lambda g, _: (layer, g, 0, 0, 0))
    y, s_new = pl.pallas_call(
        functools.partial(_gla_kernel, rows=rows, chunk=t, carry=False, n_prev=0, kv_blk=None),
        grid=(groups, 1),
        in_specs=[tok, st_in] + [_weight_spec(a) for a in weights],
        out_specs=(tok, st_spec),
        out_shape=(jax.ShapeDtypeStruct((groups, rows, d), F32),
                   jax.ShapeDtypeStruct((bsz, H_A, DK_A, DV_A), F32)),
        scratch_shapes=_gla_scratch(rows, t),
        compiler_params=pltpu.CompilerParams(
            dimension_semantics=("parallel", "arbitrary"), vmem_limit_bytes=VMEM_LIMIT),
        name="gla_sample_layer",
    )(x.reshape(groups, rows, d), states, *[_weight_arg(a) for a in weights])
    return y.reshape(bsz, t, d), s_new


def _kv_kernel(x_ref, cos_ref, sin_ref, nkv_ref, wdc_ref, wdr_ref, nlat_ref, ckv_ref, kr_ref):
    lat, rr = _kv_down(x_ref[...], nkv_ref, wdc_ref, wdr_ref)
    ckv, kr2 = _kv_finish(lat, rr, cos_ref[...], sin_ref[...], nlat_ref)
    ckv_ref[...] = ckv
    kr_ref[...] = kr2[:, :ROPE]


def _shared_kv_latent(x, cos2, sin2, nkv, wdc, wdr, nlat):
    n, d = x.shape
    args = [x, cos2, sin2, nkv, wdc, wdr, nlat]
    return pl.pallas_call(
        _kv_kernel,
        grid=(1,),
        in_specs=[_const_spec(a.shape) for a in args],
        out_specs=(_const_spec((n, KV_LORA)), _const_spec((n, ROPE))),
        out_shape=(jax.ShapeDtypeStruct((n, KV_LORA), F32), jax.ShapeDtypeStruct((n, ROPE), F32)),
        compiler_params=pltpu.CompilerParams(
            dimension_semantics=("arbitrary",), vmem_limit_bytes=VMEM_LIMIT),
        name="shared_kv",
    )(*args)


def _mla_q(x, cos, sin, npre_ref, win_ref, qn_ref, wq_ref, rope_w):
    hb = _rms(x, npre_ref[...]).astype(BF16)
    proj = _dot(hb, win_ref[...])
    qn = _rms(proj[:, :Q_LORA], qn_ref[...]).astype(BF16)
    qq = _dot(qn, wq_ref[...])
    scale = (NOPE + ROPE) ** -0.5
    n0, n1 = H_B * NOPE, H_B * NOPE + H_B * rope_w
    q_nope = (qq[:, :n0] * scale).astype(BF16)
    q_rope = ((qq[:, n0:n1] * cos + qq[:, n1:] * sin) * scale).astype(BF16)
    return q_nope, q_rope, proj[:, Q_LORA:]


ACC_ROWS = V_B + 16


def _mla_kernel(x_ref, kn_ref, kr_ref, vt_ref, cos_ref, sin_ref, npre_ref, win_ref, qn_ref, wqt_ref,
                wout_ref, npost_ref, y_ref, q_sc, m_sc, acc_sc, alpha_sc, p_sc, *, tq, tk):
    i = pl.program_id(1)
    c = (NOPE + ROPE) ** -0.5 * LOG2E
    n0, n1 = H_B * NOPE, H_B * (NOPE + ROPE)
    x = x_ref[0]
    hbs = {r: _rms(x[r:r + tk], npre_ref[...]).astype(BF16) for r in range(0, tq, tk)}
    qlat = jnp.concatenate([_dot(hbs[r], win_ref[:, :Q_LORA]) for r in hbs], axis=0)
    gates = {r: _dot(hbs[r], win_ref[:, Q_LORA:]) for r in hbs}
    qnt = _rms(qlat, qn_ref[...]).T.astype(BF16)
    qqt = _dot(wqt_ref[...], qnt)
    q_sc[:n0, :] = (qqt[:n0] * c).astype(BF16)
    cos, sin = cos_ref[...], sin_ref[...]
    for h in range(H_B):
        r0 = n0 + h * ROPE
        x1, x2 = qqt[r0:r0 + HALF], qqt[r0 + HALF:r0 + ROPE]
        q_sc[r0:r0 + HALF, :] = ((x1 * cos - x2 * sin) * c).astype(BF16)
        q_sc[r0 + HALF:r0 + ROPE, :] = ((x2 * cos + x1 * sin) * c).astype(BF16)
    row = lax.broadcasted_iota(jnp.int32, (ROPE, tq), 0)
    q_chunk = jnp.right_shift(i * tq + lax.broadcasted_iota(jnp.int32, (ROPE, tq), 1), CHUNK.bit_length() - 1)
    q_sc[n1:, :] = jnp.where(row > q_chunk, MASKED, 0.0).astype(BF16)
    m_sc[...] = jnp.full(m_sc.shape, -jnp.inf, F32)
    acc_sc[...] = jnp.zeros(acc_sc.shape, F32)

    ones = jnp.ones((ACC_ROWS - V_B, tk), BF16)

    def scores(j, h, lo, hi):
        k0 = pl.multiple_of(j * tk, tk)
        kb = jnp.concatenate(
            [kn_ref[0, pl.ds(k0, tk), h * NOPE:(h + 1) * NOPE], kr_ref[0, pl.ds(k0, tk), :]], axis=1)
        qt = jnp.concatenate(
            [q_sc[h * NOPE:(h + 1) * NOPE, lo:hi], q_sc[n0 + h * ROPE:n0 + (h + 1) * ROPE, lo:hi],
             q_sc[n1:, lo:hi]], axis=0)
        return _dot(kb, qt)

    def softmax(s, h, slot, lo, hi):
        m_prev = m_sc[h, :, lo:hi]
        m_new = jnp.maximum(m_prev, jnp.max(s, axis=0, keepdims=True))
        alpha_sc[slot, h, :, lo:hi] = jnp.exp2(m_prev - m_new)
        p_sc[slot, h, :, lo:hi] = jnp.exp2(s - m_new).astype(BF16)
        m_sc[h, :, lo:hi] = m_new

    def values(j, h, slot, lo, hi):
        vt = jnp.concatenate([vt_ref[0, j, h * V_B:(h + 1) * V_B, :], ones], axis=0)
        acc_sc[h, :, lo:hi] = (alpha_sc[slot, h, :, lo:hi] * acc_sc[h, :, lo:hi]
                               + _dot(vt, p_sc[slot, h, :, lo:hi]))

    for h in range(H_B):
        softmax(scores(0, h, 0, tq), h, 0, 0, tq)

    def block(j, slot_new, slot_old, lo=0):
        s = []
        for h in range(H_B):
            s.append(scores(j, h, lo, tq))
            values(j - 1, h, slot_old, 0, tq)
        for h in range(H_B):
            softmax(s[h], h, slot_new, lo, tq)

    def pair(jj, carry):
        block(2 * jj + 1, 1, 0)
        block(2 * jj + 2, 0, 1)
        return carry

    def tail_odd(last, lo=0):
        block(last, 1, 0, lo)
        for h in range(H_B):
            values(last, h, 1, lo, tq)

    def tail_even(last):
        for h in range(H_B):
            values(last, h, 0, 0, tq)

    if tq == 2 * tk:
        lax.fori_loop(0, i, pair, 0)
        tail_odd(2 * i + 1, tk)
    else:
        lax.fori_loop(0, i // 2, pair, 0)
        odd = lax.rem(i, 2) == 1
        pl.when(odd)(lambda: tail_odd(i))
        pl.when(jnp.logical_not(odd))(lambda: tail_even(i))

    for r in range(0, tq, tk):
        gate = gates[r]
        outs = []
        for h in range(H_B):
            a = acc_sc[h, :, r:r + tk]
            outs.append(a[:V_B] * (1.0 / a[V_B:V_B + 1]))
        o = jnp.concatenate(outs, axis=0).T
        og = (o * _silu(gate)).astype(BF16)
        y = _dot(og, wout_ref[...])
        y_ref[0, r:r + tk, :] = x[r:r + tk] + _rms(y, npost_ref[...])


def _mla_prompt_layer(x, kn, kr0, vt, cos8t, sin8t, npre, win, qn, wqt, wout, npost, *, tq, tk):
    bsz, tlen, d = x.shape
    assert tq in (tk, 2 * tk) and tk % CHUNK == 0 and vt.shape[3] == tk
    tok = pl.BlockSpec((1, tq, d), lambda b, t: (b, t, 0))
    full = lambda w: pl.BlockSpec((1, tlen, w), lambda b, t: (b, 0, 0))
    tab = pl.BlockSpec((cos8t.shape[0], tq), lambda b, t: (0, t))
    weights = [npre, win, qn, wqt, wout, npost]
    return pl.pallas_call(
        functools.partial(_mla_kernel, tq=tq, tk=tk),
        grid=(bsz, tlen // tq),
        in_specs=[tok, full(kn.shape[2]), full(kr0.shape[2]),
                  pl.BlockSpec((1,) + vt.shape[1:], lambda b, t: (b, 0, 0, 0)), tab, tab]
                 + [_weight_spec(a) for a in weights],
        out_specs=tok,
        out_shape=jax.ShapeDtypeStruct((bsz, tlen, d), F32),
        scratch_shapes=[pltpu.VMEM((H_B * (NOPE + ROPE) + ROPE, tq), BF16),
                        pltpu.VMEM((H_B, 1, tq), F32),
                        pltpu.VMEM((H_B, ACC_ROWS, tq), F32),
                        pltpu.VMEM((2, H_B, 1, tq), F32),
                        pltpu.VMEM((2, H_B, tk, tq), BF16)],
        compiler_params=pltpu.CompilerParams(
            dimension_semantics=("parallel", "arbitrary"), vmem_limit_bytes=VMEM_LIMIT),
        name="mla_prompt_layer",
    )(x, kn, kr0, vt, cos8t, sin8t, *[_weight_arg(a) for a in weights])


def _mla_sample_q_kernel(x_ref, cos_ref, sin_ref, npre_ref, win_ref, qn_ref, wq_ref, wuk_ref,
                         qlat_ref, qr_ref, gate_ref):
    q_nope, q_rope, gate = _mla_q(x_ref[...], cos_ref[...], sin_ref[...], npre_ref, win_ref, qn_ref,
                                  wq_ref, LANES)
    for h in range(H_B):
        qlat_ref[:, h * KV_LORA:(h + 1) * KV_LORA] = _dot_nt(
            q_nope[:, h * NOPE:(h + 1) * NOPE], wuk_ref[:, h * NOPE:(h + 1) * NOPE]).astype(BF16)
    qr_ref[...] = q_rope
    gate_ref[...] = gate


def _mla_sample_out_kernel(x_ref, olat_ref, gate_ref, wuv_ref, wout_ref, npost_ref, y_ref):
    o = jnp.concatenate(
        [_dot(olat_ref[:, h * KV_LORA:(h + 1) * KV_LORA], wuv_ref[:, h * V_B:(h + 1) * V_B])
         for h in range(H_B)], axis=1)
    og = (o * _silu(gate_ref[...])).astype(BF16)
    y = _dot(og, wout_ref[...])
    y_ref[...] = x_ref[...] + _rms(y, npost_ref[...])


def _mla_sample_attn_kernel(qlat_ref, qr_ref, cckv_ref, ckr_ref, nckv_ref, nkr_ref, olat_ref, *, past):
    nseq, t = qlat_ref.shape[0], qlat_ref.shape[1]
    cshift = CHUNK.bit_length() - 1

    def chunk_mask(s, k_start):
        rows = lax.broadcasted_iota(jnp.int32, (s.shape[0], 1), 0)
        cols = lax.broadcasted_iota(jnp.int32, (1, s.shape[1]), 1)
        q_chunk = jnp.right_shift(past + jnp.bitwise_and(rows, t - 1), cshift)
        k_chunk = jnp.right_shift(k_start + cols, cshift)
        return jnp.where(k_chunk <= q_chunk, s, -jnp.inf)

    c_ckv, n_ckv, s_c, s_n = {}, {}, {}, {}
    for i in range(nseq):
        qlat = jnp.concatenate([qlat_ref[i, :, h * KV_LORA:(h + 1) * KV_LORA] for h in range(H_B)], axis=0)
        qr = jnp.concatenate([qr_ref[i, :, h * LANES:h * LANES + ROPE] for h in range(H_B)], axis=0)
        c_ckv[i] = cckv_ref[i].astype(BF16)
        n_ckv[i] = nckv_ref[i].astype(BF16)
        s_c[i] = _dot_nt(qlat, c_ckv[i]) + _dot_nt(qr, ckr_ref[i].astype(BF16))
        s_n[i] = _dot_nt(qlat, n_ckv[i]) + _dot_nt(qr, nkr_ref[i].astype(BF16))
    for i in range(nseq):
        sc, sn = chunk_mask(s_c[i], 0), chunk_mask(s_n[i], past)
        m = jnp.maximum(jnp.max(sc, axis=-1, keepdims=True), jnp.max(sn, axis=-1, keepdims=True))
        p_c = jnp.exp(sc - m)
        p_n = jnp.exp(sn - m)
        l = jnp.sum(p_c, axis=-1, keepdims=True) + jnp.sum(p_n, axis=-1, keepdims=True)
        olat = ((_dot(p_c.astype(BF16), c_ckv[i]) + _dot(p_n.astype(BF16), n_ckv[i])) * (1.0 / l)).astype(BF16)
        for h in range(H_B):
            olat_ref[i, :, h * KV_LORA:(h + 1) * KV_LORA] = olat[h * t:(h + 1) * t]


def _one_step_call(body, args, out_shapes, name):
    return pl.pallas_call(
        body,
        grid=(1,),
        in_specs=[_weight_spec(a) for a in args],
        out_specs=tuple(_const_spec(s.shape) for s in out_shapes),
        out_shape=tuple(out_shapes),
        compiler_params=pltpu.CompilerParams(
            dimension_semantics=("arbitrary",), vmem_limit_bytes=VMEM_LIMIT),
        name=name,
    )(*[_weight_arg(a) for a in args])


def _mla_sample_layer(x, cache_ckv, cache_kr, new_ckv, new_kr, cos, sin, npre, win, qn, wq, wuk, wuv,
                      wout, npost):
    bsz, t, d = x.shape
    past = cache_ckv.shape[1]
    assert t & (t - 1) == 0
    n = bsz * t
    x2 = x.reshape(n, d)
    qlat, qr, gate = _one_step_call(
        _mla_sample_q_kernel, [x2, cos, sin, npre, win, qn, wq, wuk],
        [jax.ShapeDtypeStruct((n, H_B * KV_LORA), BF16), jax.ShapeDtypeStruct((n, H_B * LANES), BF16),
         jax.ShapeDtypeStruct((n, VD_B), F32)], "mla_sample_q")
    nseq = SAMPLE_ATTN_SEQS if bsz % SAMPLE_ATTN_SEQS == 0 else 1
    per_b = lambda a: pl.BlockSpec((nseq,) + a.shape[1:], lambda b: (b, 0, 0))
    attn_args = [qlat.reshape(bsz, t, -1), qr.reshape(bsz, t, -1), cache_ckv, cache_kr, new_ckv, new_kr]
    olat = pl.pallas_call(
        functools.partial(_mla_sample_attn_kernel, past=past),
        grid=(bsz // nseq,),
        in_specs=[per_b(a) for a in attn_args],
        out_specs=pl.BlockSpec((nseq, t, H_B * KV_LORA), lambda b: (b, 0, 0)),
        out_shape=jax.ShapeDtypeStruct((bsz, t, H_B * KV_LORA), BF16),
        compiler_params=pltpu.CompilerParams(
            dimension_semantics=("parallel",), vmem_limit_bytes=VMEM_LIMIT),
        name="mla_sample_attn",
    )(*attn_args)
    (y,) = _one_step_call(
        _mla_sample_out_kernel, [x2, olat.reshape(n, -1), gate, wuv, wout, npost],
        [jax.ShapeDtypeStruct((n, d), F32)], "mla_sample_out")
    return y.reshape(bsz, t, d)


def _rope_tables(pos_offset, tlen):
    inv = jnp.power(ROPE_BASE, -jnp.arange(HALF, dtype=F32) / HALF)
    ang = (pos_offset + jnp.arange(tlen)).astype(F32)[:, None] * inv[None, :]
    cos = jnp.cos(ang)
    sin = jnp.sin(ang)
    return jnp.concatenate([cos, cos], axis=1), jnp.concatenate([sin, sin], axis=1)


def _rot_cols(w):
    return jnp.concatenate([-w[..., HALF:], w[..., :HALF]], axis=-1)


def _row(a):
    return a.reshape(1, -1)


def kernel(x_prompt, x_sample, state_gla, cache_ckv, cache_krope, a_norm_pre, a_w_in, a_w_gate2, a_b_gate,
           a_head_norm, a_w_out, a_norm_post, kv_norm, kv_w_down, kv_latent_norm, kv_w_uk, kv_w_uv,
           b_norm_pre, b_w_in, b_q_norm, b_w_uq, b_w_out, b_norm_post):
    past = cache_ckv.shape[1]
    t_p, t_s = x_prompt.shape[1], x_sample.shape[1]

    a_win = a_w_in.astype(BF16)
    a_wglr = jnp.pad(a_w_in[:, :, 2 * KD_A + 2 * VD_A:],
                     ((0, 0), (0, 0), (0, 2 * LANES - GATE_RANK))).astype(BF16)
    a_wg2 = a_w_gate2.astype(BF16)
    a_wout = a_w_out.astype(BF16)
    wdc = kv_w_down[:, :KV_LORA].astype(BF16)
    wdr_cols = kv_w_down[:, KV_LORA:]
    wdr = jnp.concatenate([wdr_cols, wdr_cols, _rot_cols(wdr_cols), _rot_cols(wdr_cols)], axis=1).astype(BF16)
    wuk = kv_w_uk.astype(BF16)
    wuv = kv_w_uv.astype(BF16)
    b_win = b_w_in.astype(BF16)
    b_wout = b_w_out.astype(BF16)
    uq = b_w_uq.reshape(N_B, Q_LORA, H_B, NOPE + ROPE)
    uq_nope = uq[..., :NOPE].reshape(N_B, Q_LORA, H_B * NOPE)
    uq_rope = uq[..., NOPE:]
    uq_rot = _rot_cols(uq_rope)
    wqt_p = jnp.concatenate([uq_nope, uq_rope.reshape(N_B, Q_LORA, H_B * ROPE)],
                            axis=-1).astype(BF16).transpose(0, 2, 1)
    wuvt = wuv.T
    pad = lambda w: jnp.pad(w, ((0, 0), (0, 0), (0, 0), (0, LANES - ROPE))).reshape(N_B, Q_LORA, H_B * LANES)
    wq_s = jnp.concatenate([uq_nope, pad(uq_rope), pad(uq_rot)], axis=-1).astype(BF16)

    cos_p, sin_p = _rope_tables(0, t_p)
    cos_s, sin_s = _rope_tables(past, t_s)
    tile2 = lambda a: jnp.concatenate([a, a], axis=1)
    cos8t_p, sin8t_p = cos_p[:, :HALF].T, sin_p[:, :HALF].T
    padl = lambda a: jnp.tile(jnp.pad(a, ((0, 0), (0, LANES - ROPE))), (1, H_B))
    cos8_s, sin8_s = padl(cos_s), padl(sin_s)

    def lay(a, l):
        return _Layer(a.reshape(a.shape[0], 1, -1) if a.ndim == 2 else a, l)

    def gla_weights(l):
        return (lay(a_norm_pre, l), lay(a_win, l), lay(a_wglr, l), lay(a_wg2, l), lay(a_b_gate, l),
                lay(a_head_norm, l), lay(a_wout, l), lay(a_norm_post, l))

    assert t_p // CHUNK <= LANES - ROPE
    koh = jax.nn.one_hot(ROPE + jnp.arange(t_p) // CHUNK, LANES, dtype=F32)
    kv_args = (tile2(cos_p), tile2(sin_p), koh, _row(kv_norm), wdc, wdr, _row(kv_latent_norm), wuk, wuvt)
    x, st_p = x_prompt, None
    for l in range(N_A - 1):
        x, st_p = _gla_prompt_layer(x, st_p, gla_weights(l), rows=GLA_ROWS, chunk=CHUNK)
    x, st_p, ckv_p, kr_p, kn, kr0, vt = _gla_prompt_layer(
        x, st_p, gla_weights(N_A - 1), kv_args, rows=GLA_ROWS, chunk=CHUNK, kv_blk=ATT_K)
    for j in range(N_B):
        x = _mla_prompt_layer(x, kn, kr0, vt, cos8t_p, sin8t_p, lay(b_norm_pre, j), lay(b_win, j),
                              lay(b_q_norm, j), lay(wqt_p, j), lay(b_wout, j), lay(b_norm_post, j),
                              tq=ATT_Q, tk=ATT_K)
    y_prompt = x

    assert t_s <= CHUNK
    bsz_s = x_sample.shape[0]
    x, states = x_sample, []
    for l in range(N_A):
        x, s_new = _gla_sample_layer(x, state_gla, l, gla_weights(l), groups=SAMPLE_GROUPS)
        states.append(s_new)
    st_s = jnp.stack(states)
    per_row = lambda a: jnp.tile(tile2(a), (bsz_s, 1))
    ckv_s, kr_s = _shared_kv_latent(x.reshape(bsz_s * t_s, D_MODEL), per_row(cos_s), per_row(sin_s),
                                    _row(kv_norm), wdc, wdr, _row(kv_latent_norm))
    ckv_s = ckv_s.reshape(bsz_s, t_s, KV_LORA)
    kr_s = kr_s.reshape(bsz_s, t_s, ROPE)
    cos_rows, sin_rows = jnp.tile(cos8_s, (bsz_s, 1)), jnp.tile(sin8_s, (bsz_s, 1))
    for j in range(N_B):
        x = _mla_sample_layer(x, cache_ckv, cache_krope, ckv_s, kr_s, cos_rows, sin_rows,
                              lay(b_norm_pre, j), lay(b_win, j), lay(b_q_norm, j), lay(wq_s, j), wuk, wuv,
                              lay(b_wout, j), lay(b_norm_post, j))
    y_sample = x

    return (y_prompt, y_sample, st_p, st_s, ckv_p, kr_p, ckv_s, kr_s)
```
